```python
import jax, jax.numpy as jnp
from jax import lax
import numpy as np

D_MODEL = 1024
BATCH = 16
SEQ = 4096
DEPTH = 1

CTX_LEN = 256
GRID_W = 64
N_HEADS = 16
N_KV_HEADS = 4
HEAD_DIM = 64
Q_GROUP = N_HEADS // N_KV_HEADS
WINDOW = 128
BLOCK = 128
N_FOURIER_GROUPS = 4
FOURIER_GROUP_DIM = 128
FOURIER_WIDTH = N_FOURIER_GROUPS * FOURIER_GROUP_DIM
D_FF = 2816
ROPE_BASE = 10000.0
AXIS_ROT_DIM = HEAD_DIM // 2
N_MOD = 9
RMS_EPS = 1e-6
NEG_INF = -1e30

ATTN_Q_W = N_HEADS * HEAD_DIM
ATTN_KV_W = N_KV_HEADS * HEAD_DIM
Q_OFF = 0
K_OFF = Q_OFF + ATTN_Q_W
V_OFF = K_OFF + ATTN_KV_W
U_OFF = V_OFF + ATTN_KV_W
GA_OFF = U_OFF + FOURIER_WIDTH
GF_OFF = GA_OFF + D_MODEL
D_IN = GF_OFF + D_MODEL

kernel_name = "hybrid_dit_window_gqa_fnet_macaron"


def rms_norm(x, g):
    xf = x.astype(jnp.float32)
    y = xf * lax.rsqrt(jnp.mean(xf * xf, axis=-1, keepdims=True) + RMS_EPS)
    return (y * g.astype(jnp.float32)).astype(x.dtype)


def mod_vec(mod, idx):
    return mod[:, idx][:, None, :]


def adaln_norm(x, g, mod, base):
    h = rms_norm(x, g)
    return h * (1.0 + mod_vec(mod, base + 1)) + mod_vec(mod, base)


def swiglu(h, w_in, w_out):
    hu = h @ w_in
    a, u = hu[..., :D_FF], hu[..., D_FF:]
    return (jax.nn.silu(a) * u) @ w_out


def macaron_ffn(x, mod, base, g, w_in, w_out):
    h = adaln_norm(x, g, mod, base)
    return x + 0.5 * mod_vec(mod, base + 2) * swiglu(h, w_in, w_out)


def axial_rope_angles(n_tokens):
    rows = n_tokens // GRID_W
    row = jnp.repeat(jnp.arange(rows, dtype=jnp.float32), GRID_W)
    col = jnp.tile(jnp.arange(GRID_W, dtype=jnp.float32), rows)
    inv_freq = ROPE_BASE ** (-jnp.arange(0, AXIS_ROT_DIM, 2, dtype=jnp.float32) / AXIS_ROT_DIM)
    ang = jnp.concatenate([row[:, None] * inv_freq, col[:, None] * inv_freq], axis=-1)
    return jnp.cos(ang), jnp.sin(ang)


def apply_rope(x, cos, sin):
    xp = x.reshape(x.shape[:-1] + (HEAD_DIM // 2, 2))
    xe, xo = xp[..., 0], xp[..., 1]
    c = cos[None, :, None, :].astype(x.dtype)
    s = sin[None, :, None, :].astype(x.dtype)
    out = jnp.stack([xe * c - xo * s, xe * s + xo * c], axis=-1)
    return out.reshape(x.shape)


def split_projection(p):
    b, n = p.shape[:2]
    q = p[..., Q_OFF:K_OFF].reshape(b, n, N_HEADS, HEAD_DIM)
    k = p[..., K_OFF:V_OFF].reshape(b, n, N_KV_HEADS, HEAD_DIM)
    v = p[..., V_OFF:U_OFF].reshape(b, n, N_KV_HEADS, HEAD_DIM)
    u = p[..., U_OFF:GA_OFF]
    ga = p[..., GA_OFF:GF_OFF]
    gf = p[..., GF_OFF:D_IN]
    return q, k, v, u, ga, gf


def windowed_attention_with_context(q, k, v, k_ctx, v_ctx, sink):
    b, n = q.shape[:2]
    n_ctx = k_ctx.shape[1]
    nb = n // BLOCK
    scale = HEAD_DIM ** -0.5
    qb = (q * scale).reshape(b, nb, BLOCK, N_KV_HEADS, Q_GROUP, HEAD_DIM).transpose(1, 0, 2, 3, 4, 5)
    pad = ((0, 0), (BLOCK, BLOCK), (0, 0), (0, 0))
    k_pad = jnp.pad(k, pad)
    v_pad = jnp.pad(v, pad)
    sink_b = jnp.broadcast_to(sink.astype(jnp.float32).reshape(1, N_KV_HEADS, Q_GROUP, 1, 1),
                              (b, N_KV_HEADS, Q_GROUP, BLOCK, 1))

    def block(args):
        i, qi = args
        start = i * BLOCK
        kb = lax.dynamic_slice_in_dim(k_pad, start, 3 * BLOCK, axis=1)
        vb = lax.dynamic_slice_in_dim(v_pad, start, 3 * BLOCK, axis=1)
        qpos = start + jnp.arange(BLOCK)
        kpos = start - BLOCK + jnp.arange(3 * BLOCK)
        valid = (jnp.abs(qpos[:, None] - kpos[None, :]) <= WINDOW) & (kpos[None, :] >= 0) & (kpos[None, :] < n)
        s_loc = jnp.einsum('bqkgd,bskd->bkgqs', qi, kb).astype(jnp.float32)
        s_loc = jnp.where(valid, s_loc, NEG_INF)
        s_ctx = jnp.einsum('bqkgd,bskd->bkgqs', qi, k_ctx).astype(jnp.float32)
        logits = jnp.concatenate([s_loc, s_ctx, sink_b], axis=-1)
        p = jax.nn.softmax(logits, axis=-1).astype(v.dtype)
        o = (jnp.einsum('bkgqs,bskd->bqkgd', p[..., :3 * BLOCK], vb)
             + jnp.einsum('bkgqs,bskd->bqkgd', p[..., 3 * BLOCK:3 * BLOCK + n_ctx], v_ctx))
        return o.reshape(b, BLOCK, ATTN_Q_W)

    out = lax.map(block, (jnp.arange(nb), qb))
    return out.transpose(1, 0, 2, 3).reshape(b, n, ATTN_Q_W)


def context_attention(q, k, v, sink):
    b, L = q.shape[:2]
    qg = (q * HEAD_DIM ** -0.5).reshape(b, L, N_KV_HEADS, Q_GROUP, HEAD_DIM)
    s = jnp.einsum('bqkgd,bskd->bkgqs', qg, k).astype(jnp.float32)
    sink_b = jnp.broadcast_to(sink.astype(jnp.float32).reshape(1, N_KV_HEADS, Q_GROUP, 1, 1),
                              (b, N_KV_HEADS, Q_GROUP, L, 1))
    p = jax.nn.softmax(jnp.concatenate([s, sink_b], axis=-1), axis=-1).astype(v.dtype)
    o = jnp.einsum('bkgqs,bskd->bqkgd', p[..., :L], v)
    return o.reshape(b, L, ATTN_Q_W)


def fourier_mix(u):
    b, n = u.shape[:2]
    ug = u.astype(jnp.float32).reshape(b, n, N_FOURIER_GROUPS, FOURIER_GROUP_DIM)
    f = jnp.fft.fft2(ug, axes=(1, 3), norm="ortho")
    return jnp.real(f).astype(u.dtype).reshape(b, n, FOURIER_WIDTH)


def merge_branches(a, f, ga, gf, w_a, w_f, w_o):
    y = jax.nn.sigmoid(ga) * (a @ w_a) + jax.nn.sigmoid(gf) * (f @ w_f)
    return y @ w_o


def setup_inputs(seed: int = 0) -> dict:
    key = jax.random.key(seed)
    ks = jax.random.split(key, 24)
    f32 = jnp.float32

    def dense(k, shape, fan_in, gain=1.0):
        return jax.random.normal(k, shape, f32) * (gain * fan_in ** -0.5)

    def gain_vec(k, shape):
        return 1.0 + 0.05 * jax.random.normal(k, shape, f32)

    return {
        "x": jax.random.normal(ks[0], (BATCH, SEQ, D_MODEL), f32),
        "c": jax.random.normal(ks[1], (BATCH, D_MODEL), f32),
        "ctx": jax.random.normal(ks[2], (BATCH, CTX_LEN, D_MODEL), f32),
        "c_ctx": jax.random.normal(ks[3], (D_MODEL,), f32),
        "w_ada": dense(ks[4], (DEPTH, D_MODEL, N_MOD * D_MODEL), D_MODEL, 0.5),
        "b_ada": 0.02 * jax.random.normal(ks[5], (DEPTH, N_MOD * D_MODEL), f32),
        "g_ffn1": gain_vec(ks[6], (DEPTH, D_MODEL)),
        "w_ffn1_in": dense(ks[7], (DEPTH, D_MODEL, 2 * D_FF), D_MODEL),
        "w_ffn1_out": dense(ks[8], (DEPTH, D_FF, D_MODEL), D_FF),
        "g_mix": gain_vec(ks[9], (DEPTH, D_MODEL)),
        "w_in": dense(ks[10], (DEPTH, D_MODEL, D_IN), D_MODEL),
        "attn_sink": 0.5 * jax.random.normal(ks[11], (DEPTH, N_HEADS), f32),
        "w_attn_branch": dense(ks[12], (DEPTH, ATTN_Q_W, D_MODEL), ATTN_Q_W),
        "w_fourier_branch": dense(ks[13], (DEPTH, FOURIER_WIDTH, D_MODEL), FOURIER_WIDTH),
        "w_out": dense(ks[14], (DEPTH, D_MODEL, D_MODEL), D_MODEL),
        "g_ffn2": gain_vec(ks[15], (DEPTH, D_MODEL)),
        "w_ffn2_in": dense(ks[16], (DEPTH, D_MODEL, 2 * D_FF), D_MODEL),
        "w_ffn2_out": dense(ks[17], (DEPTH, D_FF, D_MODEL), D_FF),
        "g_final": gain_vec(ks[18], (D_MODEL,)),
    }


def reference(x, c, ctx, c_ctx, w_ada, b_ada, g_ffn1, w_ffn1_in, w_ffn1_out, g_mix, w_in,
              attn_sink, w_attn_branch, w_fourier_branch, w_out, g_ffn2, w_ffn2_in, w_ffn2_out,
              g_final):
    b, n = x.shape[:2]
    cos, sin = axial_rope_angles(n)
    cond_lat = jax.nn.silu(c)
    cond_ctx = jax.nn.silu(c_ctx)[None]

    for layer in range(DEPTH):
        last = layer == DEPTH - 1
        mod_lat = (cond_lat @ w_ada[layer] + b_ada[layer]).reshape(b, N_MOD, D_MODEL)
        mod_ctx = (cond_ctx @ w_ada[layer] + b_ada[layer]).reshape(1, N_MOD, D_MODEL)

        x = macaron_ffn(x, mod_lat, 0, g_ffn1[layer], w_ffn1_in[layer], w_ffn1_out[layer])
        ctx = macaron_ffn(ctx, mod_ctx, 0, g_ffn1[layer], w_ffn1_in[layer], w_ffn1_out[layer])

        h_lat = adaln_norm(x, g_mix[layer], mod_lat, 3)
        h_ctx = adaln_norm(ctx, g_mix[layer], mod_ctx, 3)
        w_in_l = w_in[layer]
        sink = attn_sink[layer]
        if last:
            kv_c = h_ctx @ w_in_l[:, K_OFF:U_OFF]
            k_c = kv_c[..., :ATTN_KV_W].reshape(b, -1, N_KV_HEADS, HEAD_DIM)
            v_c = kv_c[..., ATTN_KV_W:].reshape(b, -1, N_KV_HEADS, HEAD_DIM)
        else:
            q_c, k_c, v_c, u_c, ga_c, gf_c = split_projection(h_ctx @ w_in_l)
            a_c = context_attention(q_c, k_c, v_c, sink)
            f_c = fourier_mix(u_c)
            ctx_mixed = ctx + mod_vec(mod_ctx, 5) * merge_branches(
                a_c, f_c, ga_c, gf_c, w_attn_branch[layer], w_fourier_branch[layer], w_out[layer])

        q, k, v, u, ga, gf = split_projection(h_lat @ w_in_l)
        q = apply_rope(q, cos, sin)
        k = apply_rope(k, cos, sin)
        a = windowed_attention_with_context(q, k, v, k_c, v_c, sink)
        f = fourier_mix(u)
        x = x + mod_vec(mod_lat, 5) * merge_branches(
            a, f, ga, gf, w_attn_branch[layer], w_fourier_branch[layer], w_out[layer])

        x = macaron_ffn(x, mod_lat, 6, g_ffn2[layer], w_ffn2_in[layer], w_ffn2_out[layer])
        if not last:
            ctx = macaron_ffn(ctx_mixed, mod_ctx, 6, g_ffn2[layer], w_ffn2_in[layer], w_ffn2_out[layer])

    return rms_norm(x, g_final)
```

```python
import functools

import numpy as np
import jax
import jax.numpy as jnp
from jax import lax
from jax.experimental import pallas as pl
from jax.experimental.pallas import tpu as pltpu

N_HEADS = 16
N_KV_HEADS = 4
HEAD_DIM = 64
HALF_DIM = HEAD_DIM // 2
Q_GROUP = N_HEADS // N_KV_HEADS
WINDOW = 128
GRID_W = 64
N_FOURIER_GROUPS = 4
FOURIER_GROUP_DIM = 128
ROPE_BASE = 10000.0
N_MOD = 9
RMS_EPS = 1e-6
NEG_INF = -1e30

LANES = 128
MXU_DIM = 256
VMEM_LIMIT_BYTES = 56 * 1024 * 1024

FFN_ROWS = 512
FF_CHUNK = MXU_DIM
ATTN_ROWS = 1024
Q_BLOCK = 128
KEY_SPAN = 3 * Q_BLOCK
FFT_RADIX = 4

BF16 = jnp.bfloat16
F32 = jnp.float32


def _params(grid_rank):
    return pltpu.CompilerParams(dimension_semantics=("arbitrary",) * grid_rank, vmem_limit_bytes=VMEM_LIMIT_BYTES)


def _dot(a, b):
    return jnp.dot(a, b, preferred_element_type=F32)


def _dot_nt(a, b):
    return lax.dot_general(a, b, (((1,), (1,)), ((), ())), preferred_element_type=F32)


def _rms(x, g):
    return x * lax.rsqrt(jnp.mean(x * x, axis=-1, keepdims=True) + RMS_EPS) * g


def _adaln(x, g, mod_ref, base):
    shift = mod_ref[base:base + 1, :]
    scale = mod_ref[base + 1:base + 2, :]
    return _rms(x, g) * (1.0 + scale) + shift


def _resident(shape):
    zeros = (0,) * len(shape)
    return pl.BlockSpec(shape, lambda *_: zeros)


def _low_lanes(rows):
    return lax.broadcasted_iota(jnp.int32, (rows, LANES), 1) < HEAD_DIM


def _mod_kernel(c_ref, w_ref, b_ref, o_ref):
    c = c_ref[...]
    cond = (c * jax.nn.sigmoid(c)).astype(BF16)
    o_ref[...] = _dot(cond, w_ref[...].astype(BF16)) + b_ref[...]


def _modulation(cond_in, w_ada, b_ada):
    rows, d = cond_in.shape
    n_out = w_ada.shape[1]
    return pl.pallas_call(
        _mod_kernel,
        grid=(n_out // d,),
        in_specs=[
            _resident((rows, d)),
            pl.BlockSpec((d, d), lambda j: (0, j)),
            pl.BlockSpec((1, d), lambda j: (0, j)),
        ],
        out_specs=pl.BlockSpec((rows, d), lambda j: (0, j)),
        out_shape=jax.ShapeDtypeStruct((rows, n_out), F32),
        compiler_params=_params(1),
        name="modulation",
    )(cond_in, w_ada, b_ada.reshape(1, n_out))


def _ffn_kernel(x_ref, mod_ref, g_ref, win_ref, wout_ref, gfin_ref, o_ref, *, base, d_ff, final_norm):
    x = x_ref[...]
    h = _adaln(x, g_ref[...], mod_ref, base).astype(BF16)
    acc = jnp.zeros(x.shape, F32)
    for c0 in range(0, d_ff, FF_CHUNK):
        a = _dot(h, win_ref[:, c0:c0 + FF_CHUNK])
        u = _dot(h, win_ref[:, d_ff + c0:d_ff + c0 + FF_CHUNK])
        act = (a * jax.nn.sigmoid(a) * u).astype(BF16)
        acc = acc + _dot(act, wout_ref[c0:c0 + FF_CHUNK, :])
    out = x + 0.5 * mod_ref[base + 2:base + 3, :] * acc
    if final_norm:
        out = _rms(out, gfin_ref[...])
    o_ref[...] = out


def _ffn(x, mod, g, w_in, w_out, g_final, *, base, final_norm, rows):
    b, n, d = x.shape
    d_ff = w_out.shape[0]
    assert n % rows == 0 and d_ff % FF_CHUNK == 0
    per_batch_mod = mod.shape[0] == b
    mod_index = (lambda bi, si: (bi, 0, 0)) if per_batch_mod else (lambda bi, si: (0, 0, 0))
    kern = functools.partial(_ffn_kernel, base=base, d_ff=d_ff, final_norm=final_norm)
    return pl.pallas_call(
        kern,
        grid=(b, n // rows),
        in_specs=[
            pl.BlockSpec((None, rows, d), lambda bi, si: (bi, si, 0)),
            pl.BlockSpec((None, N_MOD, d), mod_index),
            _resident((1, d)),
            _resident((d, 2 * d_ff)),
            _resident((d_ff, d)),
            _resident((1, d)),
        ],
        out_specs=pl.BlockSpec((None, rows, d), lambda bi, si: (bi, si, 0)),
        out_shape=jax.ShapeDtypeStruct((b, n, d), F32),
        compiler_params=_params(2),
        name="ffn_final" if final_norm else "ffn",
    )(x, mod, g.reshape(1, d), w_in, w_out, g_final.reshape(1, d))


def _swap_halves(p, first_half):
    return jnp.where(first_half, pltpu.roll(p, LANES - HALF_DIM, 1), pltpu.roll(p, HALF_DIM, 1))


def _duplicate_heads(pair, low):
    swapped = pltpu.roll(pair, HEAD_DIM, 1)
    return jnp.where(low, pair, swapped), jnp.where(low, swapped, pair)


def _store_v_groups(v_ref, unit, dup, low):
    v_ref[:, 2 * unit * LANES:(2 * unit + 1) * LANES] = jnp.where(low, dup, 1.0).astype(v_ref.dtype)
    v_ref[:, (2 * unit + 1) * LANES:(2 * unit + 2) * LANES] = jnp.where(low, 1.0, dup).astype(v_ref.dtype)


def _inproj_kernel(x_ref, mod_ref, g_ref, w_ref, cos_ref, sin_ref,
                   q_ref, k_ref, v_ref, u_ref, ga_ref, gf_ref, *, widths):
    h = _adaln(x_ref[...], g_ref[...], mod_ref, 3).astype(BF16)
    cos = cos_ref[...]
    sin = sin_ref[...]
    rows = h.shape[0]
    lane = lax.broadcasted_iota(jnp.int32, (rows, LANES), 1)
    first_half = (lane % HEAD_DIM) < HALF_DIM
    low = lane < HEAD_DIM
    q_scale = HEAD_DIM ** -0.5

    def rope(p):
        return p * cos + _swap_halves(p, first_half) * sin

    def lane_groups(col, width):
        for c0 in range(0, width, MXU_DIM):
            p = _dot(h, w_ref[:, col + c0:col + c0 + MXU_DIM])
            for half in range(MXU_DIM // LANES):
                yield c0 // LANES + half, p[:, half * LANES:(half + 1) * LANES]

    q_w, kv_w, _, u_w, ga_w, gf_w = widths
    col = 0
    for grp, p in lane_groups(col, q_w):
        q_ref[:, grp * LANES:(grp + 1) * LANES] = (rope(p) * q_scale).astype(q_ref.dtype)
    col += q_w
    for grp, p in lane_groups(col, kv_w):
        first, second = _duplicate_heads(rope(p), low)
        k_ref[:, 2 * grp * LANES:(2 * grp + 1) * LANES] = first.astype(k_ref.dtype)
        k_ref[:, (2 * grp + 1) * LANES:(2 * grp + 2) * LANES] = second.astype(k_ref.dtype)
    col += kv_w
    for grp, p in lane_groups(col, kv_w):
        first, second = _duplicate_heads(p, low)
        _store_v_groups(v_ref, 2 * grp, first, low)
        _store_v_groups(v_ref, 2 * grp + 1, second, low)
    col += kv_w
    for grp, p in lane_groups(col, u_w):
        u_ref[:, grp * LANES:(grp + 1) * LANES] = p.astype(u_ref.dtype)
    col += u_w
    for ref, width in ((ga_ref, ga_w), (gf_ref, gf_w)):
        for grp, p in lane_groups(col, width):
            ref[:, grp * LANES:(grp + 1) * LANES] = jax.nn.sigmoid(p).astype(ref.dtype)
        col += width


def _in_projection(x, mod, g, w, cos_t, sin_t, widths, *, rows):
    b, n, d = x.shape
    q_w, kv_w, kv_w2, u_w, ga_w, gf_w = widths
    assert n % rows == 0 and kv_w == kv_w2 and all(wd % MXU_DIM == 0 for wd in widths)
    out_widths = (q_w, 2 * kv_w, 4 * kv_w, u_w, ga_w, gf_w)
    kern = functools.partial(_inproj_kernel, widths=widths)
    tok = lambda width: pl.BlockSpec((None, rows, width), lambda si, bi: (bi, si, 0))
    return pl.pallas_call(
        kern,
        grid=(n // rows, b),
        in_specs=[
            tok(d),
            pl.BlockSpec((None, N_MOD, d), lambda si, bi: (bi, 0, 0)),
            _resident((1, d)),
            _resident(w.shape),
            pl.BlockSpec((rows, LANES), lambda si, bi: (si, 0)),
            pl.BlockSpec((rows, LANES), lambda si, bi: (si, 0)),
        ],
        out_specs=[tok(width) for width in out_widths],
        out_shape=[jax.ShapeDtypeStruct((b, n, width), BF16) for width in out_widths],
        compiler_params=_params(2),
        name="in_projection",
    )(x, mod, g.reshape(1, d), w, cos_t, sin_t)


def _ctx_kv_kernel(x_ref, mod_ref, g_ref, w_ref, k_ref, v_ref):
    h = _adaln(x_ref[...], g_ref[...], mod_ref, 3).astype(BF16)
    width = k_ref.shape[-1]
    low = _low_lanes(h.shape[0])
    k_ref[...] = _dot(h, w_ref[:, :width]).astype(k_ref.dtype)
    v = _dot(h, w_ref[:, width:])
    for unit in range(width // LANES):
        _store_v_groups(v_ref, unit, v[:, unit * LANES:(unit + 1) * LANES], low)


def _ctx_kv(ctx, mod_ctx, g, w):
    b, n, d = ctx.shape
    width = w.shape[1] // 2
    tok = lambda wd: pl.BlockSpec((None, n, wd), lambda bi: (bi, 0, 0))
    return pl.pallas_call(
        _ctx_kv_kernel,
        grid=(b,),
        in_specs=[tok(d), pl.BlockSpec((None, N_MOD, d), lambda bi: (0, 0, 0)), _resident((1, d)), _resident(w.shape)],
        out_specs=[tok(width), tok(2 * width)],
        out_shape=[jax.ShapeDtypeStruct((b, n, width), BF16), jax.ShapeDtypeStruct((b, n, 2 * width), BF16)],
        compiler_params=_params(1),
        name="ctx_kv",
    )(ctx, mod_ctx, g.reshape(1, d), w)


def _attn_kernel(sink_ref, q_ref, k_ref, v_ref, kc_ref, vc_ref, o_ref, *, seq):
    tile_start = pl.program_id(1) * ATTN_ROWS
    low = _low_lanes(Q_BLOCK)
    keep_low = low.astype(BF16)
    keep_high = 1.0 - keep_low
    row = lax.broadcasted_iota(jnp.int32, (Q_BLOCK, KEY_SPAN), 0)
    colk = lax.broadcasted_iota(jnp.int32, (Q_BLOCK, KEY_SPAN), 1)
    group_of_row = lax.broadcasted_iota(jnp.int32, (Q_GROUP * Q_BLOCK, 1), 0) // Q_BLOCK
    half = Q_GROUP * Q_BLOCK // 2

    def block(sb, carry):
        q_start = tile_start + sb * Q_BLOCK
        k_start = pl.multiple_of(jnp.clip(q_start - Q_BLOCK, 0, seq - KEY_SPAN), Q_BLOCK)
        dist = colk - row + (k_start - q_start)
        bias = jnp.where(jnp.abs(dist) <= WINDOW, 0.0, NEG_INF).astype(F32)
        r0 = pl.multiple_of(sb * Q_BLOCK, Q_BLOCK)
        for j in range(N_KV_HEADS):
            ks = slice(j * LANES, (j + 1) * LANES)
            va = slice(2 * j * LANES, (2 * j + 1) * LANES)
            vb = slice((2 * j + 1) * LANES, (2 * j + 2) * LANES)
            qa = q_ref[pl.ds(r0, Q_BLOCK), va]
            qb = q_ref[pl.ds(r0, Q_BLOCK), vb]
            q4 = jnp.concatenate([qa * keep_low, qb * keep_low, qa * keep_high, qb * keep_high], axis=0)
            s_loc = _dot_nt(q4, k_ref[pl.ds(k_start, KEY_SPAN), ks])
            s_loc = (s_loc.reshape(Q_GROUP, Q_BLOCK, KEY_SPAN) + bias[None]).reshape(Q_GROUP * Q_BLOCK, KEY_SPAN)
            s_ctx = _dot_nt(q4, kc_ref[:, ks])
            sink = jnp.zeros((Q_GROUP * Q_BLOCK, 1), F32)
            for gi, head in enumerate((4 * j, 4 * j + 2, 4 * j + 1, 4 * j + 3)):
                sink = jnp.where(group_of_row == gi, sink_ref[head], sink)
            m = jnp.maximum(jnp.maximum(jnp.max(s_loc, axis=-1, keepdims=True),
                                        jnp.max(s_ctx, axis=-1, keepdims=True)), sink)
            p_loc = jnp.exp((s_loc - m).astype(BF16))
            p_ctx = jnp.exp((s_ctx - m).astype(BF16))
            sink_term = jnp.exp(sink - m)
            o_even = (_dot(p_loc[:half], v_ref[pl.ds(k_start, KEY_SPAN), va]) + _dot(p_ctx[:half], vc_ref[:, va]))
            o_odd = (_dot(p_loc[half:], v_ref[pl.ds(k_start, KEY_SPAN), vb]) + _dot(p_ctx[half:], vc_ref[:, vb]))
            for pair, out_cols in ((0, va), (1, vb)):
                rs = slice(pair * Q_BLOCK, (pair + 1) * Q_BLOCK)
                even, odd = o_even[rs], o_odd[rs]
                num = jnp.where(low, even, odd)
                den = jnp.where(low, pltpu.roll(even, HEAD_DIM, 1) + sink_term[rs],
                                pltpu.roll(odd, HEAD_DIM, 1) + sink_term[half + pair * Q_BLOCK:half + (pair + 1) * Q_BLOCK])
                o_ref[pl.ds(r0, Q_BLOCK), out_cols] = (num / den).astype(o_ref.dtype)
        return carry

    lax.fori_loop(0, ATTN_ROWS // Q_BLOCK, block, 0)


def _attention(sink, q, k2, v2, kc2, vc2):
    b, n, dq = q.shape
    n_ctx = kc2.shape[1]
    dk, dv = k2.shape[2], v2.shape[2]
    assert n % ATTN_ROWS == 0 and n >= KEY_SPAN
    kern = functools.partial(_attn_kernel, seq=n)
    per_sample = lambda rows, width: pl.BlockSpec((None, rows, width), lambda bi, si: (bi, 0, 0))
    return pl.pallas_call(
        kern,
        grid=(b, n // ATTN_ROWS),
        in_specs=[
            pl.BlockSpec(memory_space=pltpu.SMEM),
            pl.BlockSpec((None, ATTN_ROWS, dq), lambda bi, si: (bi, si, 0)),
            per_sample(n, dk), per_sample(n, dv), per_sample(n_ctx, dk), per_sample(n_ctx, dv),
        ],
        out_specs=pl.BlockSpec((None, ATTN_ROWS, dq), lambda bi, si: (bi, si, 0)),
        out_shape=jax.ShapeDtypeStruct((b, n, dq), BF16),
        compiler_params=_params(2),
        name="attention",
    )(sink, q, k2, v2, kc2, vc2)


def _fourier_kernel(u_ref, tc_ref, ts_ref, wch_ref, wseq_ref, o_ref, v_scr, y_scr, *, n2):
    gd = FOURIER_GROUP_DIM
    row_chunk = 256
    for g in range(N_FOURIER_GROUPS):
        for r0 in range(0, n2, row_chunk):
            vr = jnp.zeros((row_chunk, gd), F32)
            vi = jnp.zeros((row_chunk, gd), F32)
            for n1 in range(FFT_RADIX):
                rs = slice(n1 * n2 + r0, n1 * n2 + r0 + row_chunk)
                ug = u_ref[rs, g * gd:(g + 1) * gd].astype(F32)
                vr = vr + ug * tc_ref[rs, :]
                vi = vi - ug * ts_ref[rs, :]
            v_scr[r0:r0 + row_chunk, 2 * g * gd:(2 * g + 1) * gd] = vr.astype(BF16)
            v_scr[r0:r0 + row_chunk, (2 * g + 1) * gd:(2 * g + 2) * gd] = vi.astype(BF16)
    for g in range(N_FOURIER_GROUPS):
        y = _dot(v_scr[:, 2 * g * gd:(2 * g + 2) * gd], wch_ref[...])
        y_scr[0:n2, g * gd:(g + 1) * gd] = y[:, :gd].astype(BF16)
        y_scr[n2:2 * n2, g * gd:(g + 1) * gd] = y[:, gd:].astype(BF16)
    o_ref[...] = _dot(wseq_ref[...], y_scr[...]).astype(o_ref.dtype)


def _fourier_tables(n):
    n2 = n // FFT_RADIX
    gd = FOURIER_GROUP_DIM
    pos = np.arange(n, dtype=np.int64)
    k1 = np.arange(FFT_RADIX, dtype=np.int64)
    ang = 2.0 * np.pi * ((k1[:, None] * pos[None, :]) % n) / n
    tw_cos = np.broadcast_to(np.cos(ang)[:, :, None], (FFT_RADIX, n, LANES)).astype(np.float32)
    tw_sin = np.broadcast_to(np.sin(ang)[:, :, None], (FFT_RADIX, n, LANES)).astype(np.float32)
    ch = np.arange(gd, dtype=np.int64)
    ang_c = 2.0 * np.pi * ((ch[:, None] * ch[None, :]) % gd) / gd
    cc, sc = np.cos(ang_c), np.sin(ang_c)
    w_ch = np.block([[cc, -sc], [sc, cc]]).astype(np.float32)
    kk = np.arange(n2, dtype=np.int64)
    ang_s = 2.0 * np.pi * ((kk[:, None] * kk[None, :]) % n2) / n2
    scale = 1.0 / np.sqrt(float(n) * gd)
    w_seq = (np.concatenate([np.cos(ang_s), np.sin(ang_s)], axis=1) * scale).astype(np.float32)
    return jnp.asarray(tw_cos), jnp.asarray(tw_sin), jnp.asarray(w_ch), jnp.asarray(w_seq)


def _fourier(u):
    b, n, width = u.shape
    assert n % (FFT_RADIX * 256) == 0
    n2 = n // FFT_RADIX
    tw_cos, tw_sin, w_ch, w_seq = _fourier_tables(n)
    kern = functools.partial(_fourier_kernel, n2=n2)
    return pl.pallas_call(
        kern,
        grid=(FFT_RADIX, b),
        in_specs=[
            pl.BlockSpec((None, n, width), lambda k1, bi: (bi, 0, 0)),
            pl.BlockSpec((None, n, LANES), lambda k1, bi: (k1, 0, 0)),
            pl.BlockSpec((None, n, LANES), lambda k1, bi: (k1, 0, 0)),
            _resident(w_ch.shape),
            _resident(w_seq.shape),
        ],
        out_specs=pl.BlockSpec((None, n2, width), lambda k1, bi: (bi, 0, k1)),
        out_shape=jax.ShapeDtypeStruct((b, n2, FFT_RADIX * width), BF16),
        scratch_shapes=[pltpu.VMEM((n2, 2 * width), BF16), pltpu.VMEM((2 * n2, width), BF16)],
        compiler_params=_params(2),
        name="fourier",
    )(u, tw_cos, tw_sin, w_ch.astype(BF16), w_seq.astype(BF16))


def _merge_kernel(x_ref, mod_ref, a_ref, f_ref, ga_ref, gf_ref, wa_ref, wf_ref, wo_ref, o_ref, fw_scr):
    quarter, width = f_ref.shape[0], wf_ref.shape[0]
    groups = fw_scr.shape[0]
    for k1 in range(FFT_RADIX):
        fw = _dot(f_ref[:, k1 * width:(k1 + 1) * width], wf_ref[...])
        for grp in range(groups):
            fw_scr[grp, pl.ds(k1, quarter, stride=FFT_RADIX), :] = fw[:, grp * LANES:(grp + 1) * LANES]
    fw = jnp.concatenate([fw_scr[grp] for grp in range(groups)], axis=-1)
    y = ga_ref[...].astype(F32) * _dot(a_ref[...], wa_ref[...]) + gf_ref[...].astype(F32) * fw
    o_ref[...] = x_ref[...] + mod_ref[5:6, :] * _dot(y.astype(BF16), wo_ref[...])


def _merge(x, mod, a, f, ga, gf, w_a, w_f, w_o, *, rows):
    b, n, d = x.shape
    assert rows % FFT_RADIX == 0
    tok = lambda width: pl.BlockSpec((None, rows, width), lambda bi, si: (bi, si, 0))
    return pl.pallas_call(
        _merge_kernel,
        grid=(b, n // rows),
        in_specs=[
            tok(d),
            pl.BlockSpec((None, N_MOD, d), lambda bi, si: (bi, 0, 0)),
            tok(a.shape[2]),
            pl.BlockSpec((None, rows // FFT_RADIX, f.shape[2]), lambda bi, si: (bi, si, 0)),
            tok(d), tok(d),
            _resident(w_a.shape), _resident(w_f.shape), _resident(w_o.shape),
        ],
        out_specs=tok(d),
        out_shape=jax.ShapeDtypeStruct((b, n, d), F32),
        scratch_shapes=[pltpu.VMEM((d // LANES, rows, LANES), F32)],
        compiler_params=_params(2),
        name="merge",
    )(x, mod, a, f, ga, gf, w_a, w_f, w_o)


def _rope_tables(n):
    rows = n // GRID_W
    row = jnp.repeat(jnp.arange(rows, dtype=F32), GRID_W)
    col = jnp.tile(jnp.arange(GRID_W, dtype=F32), rows)
    inv_freq = ROPE_BASE ** (-jnp.arange(0, HALF_DIM, 2, dtype=F32) / HALF_DIM)
    ang = jnp.concatenate([row[:, None] * inv_freq, col[:, None] * inv_freq], axis=-1)
    cos, sin = jnp.cos(ang), jnp.sin(ang)
    reps = LANES // HEAD_DIM
    cos_t = jnp.tile(jnp.concatenate([cos, cos], axis=-1), (1, reps))
    sin_t = jnp.tile(jnp.concatenate([-sin, sin], axis=-1), (1, reps))
    return cos_t, sin_t


def _projection_columns(d_model, fourier_width):
    q_w = N_HEADS * HEAD_DIM
    kv_w = N_KV_HEADS * HEAD_DIM
    k_off, v_off = q_w, q_w + kv_w
    u_off = v_off + kv_w
    split = np.concatenate([np.arange(0, HEAD_DIM, 2), np.arange(1, HEAD_DIM, 2)])
    q_cols = np.concatenate([h * HEAD_DIM + split for h in range(N_HEADS)])
    k_cols = np.concatenate([k_off + j * HEAD_DIM + split for j in range(N_KV_HEADS)])
    rest = np.arange(v_off, u_off + fourier_width + 2 * d_model)
    lat_cols = np.concatenate([q_cols, k_cols, rest])
    widths = (q_w, kv_w, kv_w, fourier_width, d_model, d_model)
    k_twice = np.concatenate([np.tile(k_off + j * HEAD_DIM + split, 2) for j in range(N_KV_HEADS)])
    v_twice = np.concatenate([np.tile(v_off + j * HEAD_DIM + np.arange(HEAD_DIM), 2) for j in range(N_KV_HEADS)])
    return lat_cols, widths, np.concatenate([k_twice, v_twice])


def kernel(x, c, ctx, c_ctx, w_ada, b_ada, g_ffn1, w_ffn1_in, w_ffn1_out, g_mix, w_in, attn_sink,
           w_attn_branch, w_fourier_branch, w_out, g_ffn2, w_ffn2_in, w_ffn2_out, g_final):
    b, n, d = x.shape
    depth = w_ada.shape[0]
    assert depth == 1, "single-layer problem: the last-layer path of the reference"
    layer = 0
    fourier_width = N_FOURIER_GROUPS * FOURIER_GROUP_DIM

    cond_rows = 8 * ((b + 1 + 7) // 8)
    cond_in = jnp.zeros((cond_rows, d), F32).at[:b].set(c).at[b].set(c_ctx)
    mod = _modulation(cond_in, w_ada[layer], b_ada[layer]).reshape(cond_rows, N_MOD, d)
    mod_lat, mod_ctx = mod[:b], mod[b:b + 1]

    bf = lambda w: w.astype(BF16)
    lat_cols, widths, ctx_cols = _projection_columns(d, fourier_width)
    w_in_l = w_in[layer]
    w_lat = bf(jnp.take(w_in_l, jnp.asarray(lat_cols), axis=1))
    w_ctx = bf(jnp.take(w_in_l, jnp.asarray(ctx_cols), axis=1))
    cos_t, sin_t = _rope_tables(n)

    w1_in, w1_out = bf(w_ffn1_in[layer]), bf(w_ffn1_out[layer])
    x1 = _ffn(x, mod_lat, g_ffn1[layer], w1_in, w1_out, g_final, base=0, final_norm=False, rows=FFN_ROWS)
    ctx1 = _ffn(ctx, mod_ctx, g_ffn1[layer], w1_in, w1_out, g_final, base=0, final_norm=False,
                rows=ctx.shape[1])

    q, k2, v2, u, ga, gf = _in_projection(x1, mod_lat, g_mix[layer], w_lat, cos_t, sin_t, widths, rows=FFN_ROWS)
    kc2, vc2 = _ctx_kv(ctx1, mod_ctx, g_mix[layer], w_ctx)

    a = _attention(attn_sink[layer], q, k2, v2, kc2, vc2)
    f = _fourier(u)

    x2 = _merge(x1, mod_lat, a, f, ga, gf, bf(w_attn_branch[layer]), bf(w_fourier_branch[layer]),
                bf(w_out[layer]), rows=FFN_ROWS)
    return _ffn(x2, mod_lat, g_ffn2[layer], bf(w_ffn2_in[layer]), bf(w_ffn2_out[layer]), g_final,
                base=6, final_norm=True, rows=FFN_ROWS)
```

```python
import functools

import numpy as np
import jax
import jax.numpy as jnp
from jax import lax
from jax.experimental import pallas as pl
from jax.experimental.pallas import tpu as pltpu

N_HEADS = 16
N_KV_HEADS = 4
HEAD_DIM = 64
Q_GROUP = N_HEADS // N_KV_HEADS
WINDOW = 128
GRID_W = 64
N_FOURIER_GROUPS = 4
FOURIER_GROUP_DIM = 128
ROPE_BASE = 10000.0
N_MOD = 9
RMS_EPS = 1e-6
NEG_INF = -1e30

LANES = 128
MXU_DIM = 256
BF16_SUBLANES = 16
VMEM_LIMIT_BYTES = 56 * 1024 * 1024

TOKEN_ROWS = 1024
FF_CHUNK = MXU_DIM
ATTN_ROWS = 1024
Q_BLOCK = 128
ATTN_UNROLL = 4
KEY_BLOCKS = 3
KEY_SPAN = KEY_BLOCKS * Q_BLOCK
VT_ROWS = HEAD_DIM + BF16_SUBLANES
FFT_RADIX = 4

BF16 = jnp.bfloat16
F32 = jnp.float32


def _params(grid_rank):
    return pltpu.CompilerParams(dimension_semantics=("arbitrary",) * grid_rank, vmem_limit_bytes=VMEM_LIMIT_BYTES)


def _dot(a, b):
    return jnp.dot(a, b, preferred_element_type=F32)


def _dot_nt(a, b):
    return lax.dot_general(a, b, (((1,), (1,)), ((), ())), preferred_element_type=F32)


def _rms(x, g):
    return x * lax.rsqrt(jnp.mean(x * x, axis=-1, keepdims=True) + RMS_EPS) * g


def _adaln(x, g, mod_ref, base):
    shift = mod_ref[base:base + 1, :]
    scale = mod_ref[base + 1:base + 2, :]
    return _rms(x, g) * (1.0 + scale) + shift


def _resident(shape):
    zeros = (0,) * len(shape)
    return pl.BlockSpec(shape, lambda *_: zeros)


def _lane_iota(rows):
    return lax.broadcasted_iota(jnp.int32, (rows, LANES), 1)


def _mod_kernel(c_ref, w_ref, b_ref, o_ref):
    c = c_ref[...]
    cond = (c * jax.nn.sigmoid(c)).astype(BF16)
    o_ref[...] = _dot(cond, w_ref[...].astype(BF16)) + b_ref[...]


def _modulation(cond_in, w_ada, b_ada):
    rows, d = cond_in.shape
    n_out = w_ada.shape[1]
    return pl.pallas_call(
        _mod_kernel,
        grid=(n_out // d,),
        in_specs=[
            _resident((rows, d)),
            pl.BlockSpec((d, d), lambda j: (0, j)),
            pl.BlockSpec((1, d), lambda j: (0, j)),
        ],
        out_specs=pl.BlockSpec((rows, d), lambda j: (0, j)),
        out_shape=jax.ShapeDtypeStruct((rows, n_out), F32),
        compiler_params=_params(1),
        name="modulation",
    )(cond_in, w_ada, b_ada.reshape(1, n_out))


def _ffn_kernel(x_ref, mod_ref, g_ref, win_ref, wout_ref, gfin_ref, o_ref, *, base, d_ff, final_norm):
    x = x_ref[...]
    h = _adaln(x, g_ref[...], mod_ref, base).astype(BF16)
    acc = jnp.zeros(x.shape, F32)
    for c0 in range(0, d_ff, FF_CHUNK):
        a = _dot(h, win_ref[:, c0:c0 + FF_CHUNK])
        u = _dot(h, win_ref[:, d_ff + c0:d_ff + c0 + FF_CHUNK])
        act = (a * jax.nn.sigmoid(a) * u).astype(BF16)
        acc = acc + _dot(act, wout_ref[c0:c0 + FF_CHUNK, :])
    out = x + 0.5 * mod_ref[base + 2:base + 3, :] * acc
    if final_norm:
        out = _rms(out, gfin_ref[...])
    o_ref[...] = out


def _ffn(x, mod, g, w_in, w_out, g_final, *, base, final_norm, rows):
    b, n, d = x.shape
    d_ff = w_out.shape[0]
    assert n % rows == 0 and d_ff % FF_CHUNK == 0
    per_batch_mod = mod.shape[0] == b
    mod_index = (lambda bi, si: (bi, 0, 0)) if per_batch_mod else (lambda bi, si: (0, 0, 0))
    kern = functools.partial(_ffn_kernel, base=base, d_ff=d_ff, final_norm=final_norm)
    return pl.pallas_call(
        kern,
        grid=(b, n // rows),
        in_specs=[
            pl.BlockSpec((None, rows, d), lambda bi, si: (bi, si, 0)),
            pl.BlockSpec((None, N_MOD, d), mod_index),
            _resident((1, d)),
            _resident((d, 2 * d_ff)),
            _resident((d_ff, d)),
            _resident((1, d)),
        ],
        out_specs=pl.BlockSpec((None, rows, d), lambda bi, si: (bi, si, 0)),
        out_shape=jax.ShapeDtypeStruct((b, n, d), F32),
        compiler_params=_params(2),
        name="ffn_final" if final_norm else "ffn",
    )(x, mod, g.reshape(1, d), w_in, w_out, g_final.reshape(1, d))


def _rope(p, cos, sin, even_lane):
    partner = jnp.where(even_lane, pltpu.roll(p, LANES - 1, 1), pltpu.roll(p, 1, 1))
    return p * cos + partner * sin


def _duplicate_heads(pair, low):
    swapped = pltpu.roll(pair, HEAD_DIM, 1)
    return jnp.where(low, pair, swapped), jnp.where(low, swapped, pair)


def _lane_groups(h, w_ref, col, width):
    for c0 in range(0, width, MXU_DIM):
        p = _dot(h, w_ref[:, col + c0:col + c0 + MXU_DIM])
        for half in range(MXU_DIM // LANES):
            yield c0 // LANES + half, p[:, half * LANES:(half + 1) * LANES]


def _store_k(k_ref, h, w_ref, col, width, low, rope):
    for grp, p in _lane_groups(h, w_ref, col, width):
        first, second = _duplicate_heads(rope(p), low)
        k_ref[:, 2 * grp * LANES:(2 * grp + 1) * LANES] = first.astype(k_ref.dtype)
        k_ref[:, (2 * grp + 1) * LANES:(2 * grp + 2) * LANES] = second.astype(k_ref.dtype)


def _store_vt(vt_ref, h, wvt_ref):
    vt = _dot_nt(wvt_ref[...], h)
    ones = jnp.ones((VT_ROWS - HEAD_DIM, Q_BLOCK), vt_ref.dtype)
    for t in range(vt_ref.shape[0]):
        for j in range(N_KV_HEADS):
            vt_ref[t, j * VT_ROWS:j * VT_ROWS + HEAD_DIM, :] = (
                vt[j * HEAD_DIM:(j + 1) * HEAD_DIM, t * Q_BLOCK:(t + 1) * Q_BLOCK].astype(vt_ref.dtype))
            vt_ref[t, j * VT_ROWS + HEAD_DIM:(j + 1) * VT_ROWS, :] = ones


def _inproj_kernel(x_ref, mod_ref, g_ref, w_ref, wvt_ref, cos_ref, sin_ref,
                   q_ref, k_ref, vt_ref, u_ref, ga_ref, gf_ref, *, offsets):
    h = _adaln(x_ref[...], g_ref[...], mod_ref, 3).astype(BF16)
    cos = cos_ref[...]
    sin = sin_ref[...]
    lane = _lane_iota(h.shape[0])
    even_lane = (lane % 2) == 0
    low = lane < HEAD_DIM
    q_scale = HEAD_DIM ** -0.5
    rope = functools.partial(_rope, cos=cos, sin=sin, even_lane=even_lane)
    q_off, k_off, u_off, ga_off, gf_off = offsets

    for grp, p in _lane_groups(h, w_ref, q_off, q_ref.shape[1]):
        q_ref[:, grp * LANES:(grp + 1) * LANES] = (rope(p) * q_scale).astype(q_ref.dtype)
    _store_k(k_ref, h, w_ref, k_off, k_ref.shape[1] // 2, low, rope)
    _store_vt(vt_ref, h, wvt_ref)
    for grp, p in _lane_groups(h, w_ref, u_off, u_ref.shape[1]):
        u_ref[:, grp * LANES:(grp + 1) * LANES] = p.astype(u_ref.dtype)
    for ref, off in ((ga_ref, ga_off), (gf_ref, gf_off)):
        for grp, p in _lane_groups(h, w_ref, off, ref.shape[1]):
            ref[:, grp * LANES:(grp + 1) * LANES] = jax.nn.sigmoid(p).astype(ref.dtype)


def _in_projection(x, mod, g, w, w_vt, cos_t, sin_t, fourier_width, *, rows):
    b, n, d = x.shape
    q_w, kv_w = N_HEADS * HEAD_DIM, N_KV_HEADS * HEAD_DIM
    assert n % rows == 0 and rows % Q_BLOCK == 0 and kv_w % MXU_DIM == 0 and fourier_width % MXU_DIM == 0
    u_off = q_w + 2 * kv_w
    offsets = (0, q_w, u_off, u_off + fourier_width, u_off + fourier_width + d)
    kern = functools.partial(_inproj_kernel, offsets=offsets)
    tok = lambda width: pl.BlockSpec((None, rows, width), lambda si, bi: (bi, si, 0))
    key_blocks = rows // Q_BLOCK
    vt_rows = N_KV_HEADS * VT_ROWS
    out_widths = (q_w, 2 * kv_w, None, fourier_width, d, d)
    out_specs = [tok(width) if width else
                 pl.BlockSpec((None, key_blocks, vt_rows, Q_BLOCK), lambda si, bi: (bi, si, 0, 0))
                 for width in out_widths]
    out_shape = [jax.ShapeDtypeStruct((b, n, width), BF16) if width else
                 jax.ShapeDtypeStruct((b, n // Q_BLOCK, vt_rows, Q_BLOCK), BF16)
                 for width in out_widths]
    return pl.pallas_call(
        kern,
        grid=(n // rows, b),
        in_specs=[
            tok(d),
            pl.BlockSpec((None, N_MOD, d), lambda si, bi: (bi, 0, 0)),
            _resident((1, d)),
            _resident(w.shape),
            _resident(w_vt.shape),
            pl.BlockSpec((rows, LANES), lambda si, bi: (si, 0)),
            pl.BlockSpec((rows, LANES), lambda si, bi: (si, 0)),
        ],
        out_specs=out_specs,
        out_shape=out_shape,
        compiler_params=_params(2),
        name="in_projection",
    )(x, mod, g.reshape(1, d), w, w_vt, cos_t, sin_t)


def _ctx_kv_kernel(x_ref, mod_ref, g_ref, w_ref, wvt_ref, k_ref, vt_ref, *, k_off):
    h = _adaln(x_ref[...], g_ref[...], mod_ref, 3).astype(BF16)
    low = _lane_iota(h.shape[0]) < HEAD_DIM
    _store_k(k_ref, h, w_ref, k_off, k_ref.shape[1] // 2, low, lambda p: p)
    _store_vt(vt_ref, h, wvt_ref)


def _ctx_kv(ctx, mod_ctx, g, w, w_vt):
    b, n, d = ctx.shape
    kv_w = N_KV_HEADS * HEAD_DIM
    vt_rows = N_KV_HEADS * VT_ROWS
    assert n % Q_BLOCK == 0
    kern = functools.partial(_ctx_kv_kernel, k_off=N_HEADS * HEAD_DIM)
    return pl.pallas_call(
        kern,
        grid=(b,),
        in_specs=[pl.BlockSpec((None, n, d), lambda bi: (bi, 0, 0)),
                  pl.BlockSpec((None, N_MOD, d), lambda bi: (0, 0, 0)),
                  _resident((1, d)), _resident(w.shape), _resident(w_vt.shape)],
        out_specs=[pl.BlockSpec((None, n, 2 * kv_w), lambda bi: (bi, 0, 0)),
                   pl.BlockSpec((None, n // Q_BLOCK, vt_rows, Q_BLOCK), lambda bi: (bi, 0, 0, 0))],
        out_shape=[jax.ShapeDtypeStruct((b, n, 2 * kv_w), BF16),
                   jax.ShapeDtypeStruct((b, n // Q_BLOCK, vt_rows, Q_BLOCK), BF16)],
        compiler_params=_params(1),
        name="ctx_kv",
    )(ctx, mod_ctx, g.reshape(1, d), w, w_vt)


def _attn_kernel(sink_ref, q_ref, k_ref, vt_ref, kc_ref, vct_ref, o_ref, *, seq):
    tile_start = pl.program_id(1) * ATTN_ROWS
    lane = _lane_iota(Q_BLOCK)
    keep_low = (lane < HEAD_DIM).astype(BF16)
    keep_high = 1.0 - keep_low
    key_row = lax.broadcasted_iota(jnp.int32, (KEY_SPAN, Q_BLOCK), 0)
    query_col = lax.broadcasted_iota(jnp.int32, (KEY_SPAN, Q_BLOCK), 1)
    head_of_lane = lax.broadcasted_iota(jnp.int32, (1, Q_GROUP * Q_BLOCK), 1) // Q_BLOCK
    n_ctx_blocks = vct_ref.shape[0]

    def window(sb):
        q_start = tile_start + sb * Q_BLOCK
        k_start = pl.multiple_of(jnp.clip(q_start - Q_BLOCK, 0, seq - KEY_SPAN), Q_BLOCK)
        dist = key_row - query_col + (k_start - q_start)
        bias = jnp.where(jnp.abs(dist) <= WINDOW, 0.0, NEG_INF).astype(F32)
        return pl.multiple_of(sb * Q_BLOCK, Q_BLOCK), k_start, bias

    def scores(win, j):
        r0, k_start, bias = win
        ks = slice(j * LANES, (j + 1) * LANES)
        qa = q_ref[pl.ds(r0, Q_BLOCK), 2 * j * LANES:(2 * j + 1) * LANES]
        qb = q_ref[pl.ds(r0, Q_BLOCK), (2 * j + 1) * LANES:(2 * j + 2) * LANES]
        q4 = jnp.concatenate([qa * keep_low, qa * keep_high, qb * keep_low, qb * keep_high], axis=0)
        s_loc = _dot_nt(k_ref[pl.ds(k_start, KEY_SPAN), ks], q4)
        s_loc = jnp.concatenate([s_loc[:, g * Q_BLOCK:(g + 1) * Q_BLOCK] + bias for g in range(Q_GROUP)], axis=1)
        s_ctx = _dot_nt(kc_ref[:, ks], q4)
        return s_loc, s_ctx

    def block(step, carry):
        wins = [window(step * ATTN_UNROLL + i) for i in range(ATTN_UNROLL)]
        units = [(win, j) for win in wins for j in range(N_KV_HEADS)]
        pending = scores(*units[0])
        for idx, (win, j) in enumerate(units):
            r0, k_start, _ = win
            k_block = k_start // Q_BLOCK
            s_loc, s_ctx = pending
            if idx + 1 < len(units):
                pending = scores(*units[idx + 1])
            vs = slice(j * VT_ROWS, (j + 1) * VT_ROWS)
            sink = jnp.zeros((1, Q_GROUP * Q_BLOCK), F32)
            for gi in range(Q_GROUP):
                sink = jnp.where(head_of_lane == gi, sink_ref[j * Q_GROUP + gi], sink)
            m = jnp.maximum(jnp.maximum(jnp.max(s_loc, axis=0, keepdims=True),
                                        jnp.max(s_ctx, axis=0, keepdims=True)), sink)
            p_all = jnp.concatenate([jnp.exp((s_loc - m).astype(BF16)), jnp.exp((s_ctx - m).astype(BF16))], axis=0)
            vt_all = jnp.concatenate([vt_ref[k_block + t, vs, :] for t in range(KEY_BLOCKS)]
                                     + [vct_ref[t, vs, :] for t in range(n_ctx_blocks)], axis=1)
            out_t = _dot(vt_all, p_all)
            denom = out_t[HEAD_DIM:HEAD_DIM + 1, :] + jnp.exp(sink - m)
            vals = out_t[:HEAD_DIM, :] / denom
            for pair in range(Q_GROUP // 2):
                two_heads = jnp.concatenate([vals[:, (2 * pair) * Q_BLOCK:(2 * pair + 1) * Q_BLOCK],
                                             vals[:, (2 * pair + 1) * Q_BLOCK:(2 * pair + 2) * Q_BLOCK]], axis=0)
                o_ref[pl.ds(r0, Q_BLOCK), (2 * j + pair) * LANES:(2 * j + pair + 1) * LANES] = (
                    two_heads.T.astype(o_ref.dtype))
        return carry

    lax.fori_loop(0, ATTN_ROWS // (Q_BLOCK * ATTN_UNROLL), block, 0)


def _attention(sink, q, k2, vt, kc2, vct):
    b, n, dq = q.shape
    n_ctx, dk = kc2.shape[1], k2.shape[2]
    assert n % ATTN_ROWS == 0 and n >= KEY_SPAN and n_ctx % Q_BLOCK == 0
    kern = functools.partial(_attn_kernel, seq=n)
    per_sample = lambda arr: pl.BlockSpec((None,) + arr.shape[1:], lambda bi, si: (bi,) + (0,) * (arr.ndim - 1))
    return pl.pallas_call(
        kern,
        grid=(b, n // ATTN_ROWS),
        in_specs=[
            pl.BlockSpec(memory_space=pltpu.SMEM),
            pl.BlockSpec((None, ATTN_ROWS, dq), lambda bi, si: (bi, si, 0)),
            per_sample(k2), per_sample(vt), per_sample(kc2), per_sample(vct),
        ],
        out_specs=pl.BlockSpec((None, ATTN_ROWS, dq), lambda bi, si: (bi, si, 0)),
        out_shape=jax.ShapeDtypeStruct((b, n, dq), BF16),
        compiler_params=_params(2),
        name="attention",
    )(sink, q, k2, vt, kc2, vct)


def _fourier_kernel(u_ref, tc_ref, ts_ref, wch_ref, wseq_ref, o_ref, v_scr, y_scr, *, n2):
    gd = FOURIER_GROUP_DIM
    row_chunk = 256
    for g in range(N_FOURIER_GROUPS):
        for r0 in range(0, n2, row_chunk):
            vr = jnp.zeros((row_chunk, gd), F32)
            vi = jnp.zeros((row_chunk, gd), F32)
            for n1 in range(FFT_RADIX):
                rs = slice(n1 * n2 + r0, n1 * n2 + r0 + row_chunk)
                ug = u_ref[rs, g * gd:(g + 1) * gd].astype(F32)
                vr = vr + ug * tc_ref[rs, :]
                vi = vi - ug * ts_ref[rs, :]
            v_scr[r0:r0 + row_chunk, 2 * g * gd:(2 * g + 1) * gd] = vr.astype(BF16)
            v_scr[r0:r0 + row_chunk, (2 * g + 1) * gd:(2 * g + 2) * gd] = vi.astype(BF16)
    for g in range(N_FOURIER_GROUPS):
        y = _dot(v_scr[:, 2 * g * gd:(2 * g + 2) * gd], wch_ref[...])
        y_scr[0:n2, g * gd:(g + 1) * gd] = y[:, :gd].astype(BF16)
        y_scr[n2:2 * n2, g * gd:(g + 1) * gd] = y[:, gd:].astype(BF16)
    o_ref[...] = _dot(wseq_ref[...], y_scr[...]).astype(o_ref.dtype)


def _fourier_tables(n):
    n2 = n // FFT_RADIX
    gd = FOURIER_GROUP_DIM
    pos = np.arange(n, dtype=np.int64)
    k1 = np.arange(FFT_RADIX, dtype=np.int64)
    ang = 2.0 * np.pi * ((k1[:, None] * pos[None, :]) % n) / n
    tw_cos = np.broadcast_to(np.cos(ang)[:, :, None], (FFT_RADIX, n, LANES)).astype(np.float32)
    tw_sin = np.broadcast_to(np.sin(ang)[:, :, None], (FFT_RADIX, n, LANES)).astype(np.float32)
    ch = np.arange(gd, dtype=np.int64)
    ang_c = 2.0 * np.pi * ((ch[:, None] * ch[None, :]) % gd) / gd
    cc, sc = np.cos(ang_c), np.sin(ang_c)
    w_ch = np.block([[cc, -sc], [sc, cc]]).astype(np.float32)
    kk = np.arange(n2, dtype=np.int64)
    ang_s = 2.0 * np.pi * ((kk[:, None] * kk[None, :]) % n2) / n2
    scale = 1.0 / np.sqrt(float(n) * gd)
    w_seq = (np.concatenate([np.cos(ang_s), np.sin(ang_s)], axis=1) * scale).astype(np.float32)
    return jnp.asarray(tw_cos), jnp.asarray(tw_sin), jnp.asarray(w_ch), jnp.asarray(w_seq)


def _fourier(u):
    b, n, width = u.shape
    assert n % (FFT_RADIX * 256) == 0
    n2 = n // FFT_RADIX
    tw_cos, tw_sin, w_ch, w_seq = _fourier_tables(n)
    kern = functools.partial(_fourier_kernel, n2=n2)
    return pl.pallas_call(
        kern,
        grid=(FFT_RADIX, b),
        in_specs=[
            pl.BlockSpec((None, n, width), lambda k1, bi: (bi, 0, 0)),
            pl.BlockSpec((None, n, LANES), lambda k1, bi: (k1, 0, 0)),
            pl.BlockSpec((None, n, LANES), lambda k1, bi: (k1, 0, 0)),
            _resident(w_ch.shape),
            _resident(w_seq.shape),
        ],
        out_specs=pl.BlockSpec((None, n2, width), lambda k1, bi: (bi, 0, k1)),
        out_shape=jax.ShapeDtypeStruct((b, n2, FFT_RADIX * width), BF16),
        scratch_shapes=[pltpu.VMEM((n2, 2 * width), BF16), pltpu.VMEM((2 * n2, width), BF16)],
        compiler_params=_params(2),
        name="fourier",
    )(u, tw_cos, tw_sin, w_ch.astype(BF16), w_seq.astype(BF16))


def _merge_kernel(x_ref, mod_ref, a_ref, f_ref, ga_ref, gf_ref, wa_ref, wf_ref, wo_ref, o_ref, fw_scr):
    quarter, width = f_ref.shape[0], wf_ref.shape[0]
    groups = fw_scr.shape[0]
    for k1 in range(FFT_RADIX):
        fw = _dot(f_ref[:, k1 * width:(k1 + 1) * width], wf_ref[...])
        for grp in range(groups):
            fw_scr[grp, pl.ds(k1, quarter, stride=FFT_RADIX), :] = fw[:, grp * LANES:(grp + 1) * LANES]
    fw = jnp.concatenate([fw_scr[grp] for grp in range(groups)], axis=-1)
    y = ga_ref[...].astype(F32) * _dot(a_ref[...], wa_ref[...]) + gf_ref[...].astype(F32) * fw
    o_ref[...] = x_ref[...] + mod_ref[5:6, :] * _dot(y.astype(BF16), wo_ref[...])


def _merge(x, mod, a, f, ga, gf, w_a, w_f, w_o, *, rows):
    b, n, d = x.shape
    assert rows % FFT_RADIX == 0
    tok = lambda width: pl.BlockSpec((None, rows, width), lambda bi, si: (bi, si, 0))
    return pl.pallas_call(
        _merge_kernel,
        grid=(b, n // rows),
        in_specs=[
            tok(d),
            pl.BlockSpec((None, N_MOD, d), lambda bi, si: (bi, 0, 0)),
            tok(a.shape[2]),
            pl.BlockSpec((None, rows // FFT_RADIX, f.shape[2]), lambda bi, si: (bi, si, 0)),
            tok(d), tok(d),
            _resident(w_a.shape), _resident(w_f.shape), _resident(w_o.shape),
        ],
        out_specs=tok(d),
        out_shape=jax.ShapeDtypeStruct((b, n, d), F32),
        scratch_shapes=[pltpu.VMEM((d // LANES, rows, LANES), F32)],
        compiler_params=_params(2),
        name="merge",
    )(x, mod, a, f, ga, gf, w_a, w_f, w_o)


def _rope_tables(n):
    rows = n // GRID_W
    half = HEAD_DIM // 2
    row = jnp.repeat(jnp.arange(rows, dtype=F32), GRID_W)
    col = jnp.tile(jnp.arange(GRID_W, dtype=F32), rows)
    inv_freq = ROPE_BASE ** (-jnp.arange(0, half, 2, dtype=F32) / half)
    ang = jnp.concatenate([row[:, None] * inv_freq, col[:, None] * inv_freq], axis=-1)
    cos, sin = jnp.cos(ang), jnp.sin(ang)
    reps = LANES // HEAD_DIM
    cos_t = jnp.tile(jnp.repeat(cos, 2, axis=-1), (1, reps))
    sin_t = jnp.tile(jnp.stack([-sin, sin], axis=-1).reshape(n, HEAD_DIM), (1, reps))
    return cos_t, sin_t


def kernel(x, c, ctx, c_ctx, w_ada, b_ada, g_ffn1, w_ffn1_in, w_ffn1_out, g_mix, w_in, attn_sink,
           w_attn_branch, w_fourier_branch, w_out, g_ffn2, w_ffn2_in, w_ffn2_out, g_final):
    b, n, d = x.shape
    depth = w_ada.shape[0]
    assert depth == 1, "single-layer problem: the last-layer path of the reference"
    layer = 0
    fourier_width = N_FOURIER_GROUPS * FOURIER_GROUP_DIM

    cond_rows = 8 * ((b + 1 + 7) // 8)
    cond_in = jnp.zeros((cond_rows, d), F32).at[:b].set(c).at[b].set(c_ctx)
    mod = _modulation(cond_in, w_ada[layer], b_ada[layer]).reshape(cond_rows, N_MOD, d)
    mod_lat, mod_ctx = mod[:b], mod[b:b + 1]

    bf = lambda w: w.astype(BF16)
    w_proj = bf(w_in[layer])
    v_off = (N_HEADS + N_KV_HEADS) * HEAD_DIM
    w_vt = w_proj[:, v_off:v_off + N_KV_HEADS * HEAD_DIM].T
    cos_t, sin_t = _rope_tables(n)

    w1_in, w1_out = bf(w_ffn1_in[layer]), bf(w_ffn1_out[layer])
    x1 = _ffn(x, mod_lat, g_ffn1[layer], w1_in, w1_out, g_final, base=0, final_norm=False, rows=TOKEN_ROWS)
    ctx1 = _ffn(ctx, mod_ctx, g_ffn1[layer], w1_in, w1_out, g_final, base=0, final_norm=False,
                rows=ctx.shape[1])

    q, k2, vt, u, ga, gf = _in_projection(x1, mod_lat, g_mix[layer], w_proj, w_vt, cos_t, sin_t, fourier_width,
                                          rows=TOKEN_ROWS)
    kc2, vct = _ctx_kv(ctx1, mod_ctx, g_mix[layer], w_proj, w_vt)

    a = _attention(attn_sink[layer], q, k2, vt, kc2, vct)
    f = _fourier(u)

    x2 = _merge(x1, mod_lat, a, f, ga, gf, bf(w_attn_branch[layer]), bf(w_fourier_branch[layer]),
                bf(w_out[layer]), rows=TOKEN_ROWS)
    return _ffn(x2, mod_lat, g_ffn2[layer], bf(w_ffn2_in[layer]), bf(w_ffn2_out[layer]), g_final,
                base=6, final_norm=True, rows=TOKEN_ROWS)
```

```python
import functools

import numpy as np
import jax
import jax.numpy as jnp
from jax import lax
from jax.experimental import pallas as pl
from jax.experimental.pallas import tpu as pltpu

N_HEADS = 16
N_KV_HEADS = 4
HEAD_DIM = 64
Q_GROUP = N_HEADS // N_KV_HEADS
WINDOW = 128
GRID_W = 64
N_FOURIER_GROUPS = 4
FOURIER_GROUP_DIM = 128
ROPE_BASE = 10000.0
N_MOD = 9
RMS_EPS = 1e-6
NEG_INF = -1e30

LANES = 128
MXU_DIM = 256
BF16_SUBLANES = 16
VMEM_LIMIT_BYTES = 56 * 1024 * 1024

TOKEN_ROWS = 1024
FF_CHUNK = MXU_DIM
ATTN_ROWS = 1024
Q_BLOCK = 128
ATTN_UNROLL = 4
KEY_BLOCKS = 3
KEY_SPAN = KEY_BLOCKS * Q_BLOCK
VT_ROWS = HEAD_DIM + BF16_SUBLANES
FFT_RADIX = 4

BF16 = jnp.bfloat16
F32 = jnp.float32


def _params(grid_rank):
    return pltpu.CompilerParams(dimension_semantics=("arbitrary",) * grid_rank, vmem_limit_bytes=VMEM_LIMIT_BYTES)


def _dot(a, b):
    return jnp.dot(a, b, preferred_element_type=F32)


def _dot_nt(a, b):
    return lax.dot_general(a, b, (((1,), (1,)), ((), ())), preferred_element_type=F32)


def _rms(x, g):
    return x * lax.rsqrt(jnp.mean(x * x, axis=-1, keepdims=True) + RMS_EPS) * g


def _adaln(x, g, mod_ref, base):
    shift = mod_ref[base:base + 1, :]
    scale = mod_ref[base + 1:base + 2, :]
    return _rms(x, g) * (1.0 + scale) + shift


def _resident(shape):
    zeros = (0,) * len(shape)
    return pl.BlockSpec(shape, lambda *_: zeros)


def _lane_iota(rows):
    return lax.broadcasted_iota(jnp.int32, (rows, LANES), 1)


def _mod_kernel(c_ref, w_ref, b_ref, o_ref):
    c = c_ref[...]
    cond = (c * jax.nn.sigmoid(c)).astype(BF16)
    o_ref[...] = _dot(cond, w_ref[...].astype(BF16)) + b_ref[...]


def _modulation(cond_in, w_ada, b_ada):
    rows, d = cond_in.shape
    n_out = w_ada.shape[1]
    return pl.pallas_call(
        _mod_kernel,
        grid=(n_out // d,),
        in_specs=[
            _resident((rows, d)),
            pl.BlockSpec((d, d), lambda j: (0, j)),
            pl.BlockSpec((1, d), lambda j: (0, j)),
        ],
        out_specs=pl.BlockSpec((rows, d), lambda j: (0, j)),
        out_shape=jax.ShapeDtypeStruct((rows, n_out), F32),
        compiler_params=_params(1),
        name="modulation",
    )(cond_in, w_ada, b_ada.reshape(1, n_out))


def _ffn_kernel(x_ref, mod_ref, g_ref, win_ref, wout_ref, gfin_ref, o_ref, *, base, d_ff, final_norm):
    x = x_ref[...]
    h = _adaln(x, g_ref[...], mod_ref, base).astype(BF16)
    acc = jnp.zeros(x.shape, F32)
    for c0 in range(0, d_ff, FF_CHUNK):
        a = _dot(h, win_ref[:, c0:c0 + FF_CHUNK])
        u = _dot(h, win_ref[:, d_ff + c0:d_ff + c0 + FF_CHUNK])
        act = (a * jax.nn.sigmoid(a) * u).astype(BF16)
        acc = acc + _dot(act, wout_ref[c0:c0 + FF_CHUNK, :])
    out = x + 0.5 * mod_ref[base + 2:base + 3, :] * acc
    if final_norm:
        out = _rms(out, gfin_ref[...])
    o_ref[...] = out


def _ffn(x, mod, g, w_in, w_out, g_final, *, base, final_norm, rows):
    b, n, d = x.shape
    d_ff = w_out.shape[0]
    assert n % rows == 0 and d_ff % FF_CHUNK == 0
    per_batch_mod = mod.shape[0] == b
    mod_index = (lambda bi, si: (bi, 0, 0)) if per_batch_mod else (lambda bi, si: (0, 0, 0))
    kern = functools.partial(_ffn_kernel, base=base, d_ff=d_ff, final_norm=final_norm)
    return pl.pallas_call(
        kern,
        grid=(b, n // rows),
        in_specs=[
            pl.BlockSpec((None, rows, d), lambda bi, si: (bi, si, 0)),
            pl.BlockSpec((None, N_MOD, d), mod_index),
            _resident((1, d)),
            _resident((d, 2 * d_ff)),
            _resident((d_ff, d)),
            _resident((1, d)),
        ],
        out_specs=pl.BlockSpec((None, rows, d), lambda bi, si: (bi, si, 0)),
        out_shape=jax.ShapeDtypeStruct((b, n, d), F32),
        compiler_params=_params(2),
        name="ffn_final" if final_norm else "ffn",
    )(x, mod, g.reshape(1, d), w_in, w_out, g_final.reshape(1, d))


def _rope(p, cos, sin, even_lane):
    partner = jnp.where(even_lane, pltpu.roll(p, LANES - 1, 1), pltpu.roll(p, 1, 1))
    return p * cos + partner * sin


def _duplicate_heads(pair, low):
    swapped = pltpu.roll(pair, HEAD_DIM, 1)
    return jnp.where(low, pair, swapped), jnp.where(low, swapped, pair)


def _lane_groups(h, w_ref, col, width):
    for c0 in range(0, width, MXU_DIM):
        p = _dot(h, w_ref[:, col + c0:col + c0 + MXU_DIM])
        for half in range(MXU_DIM // LANES):
            yield c0 // LANES + half, p[:, half * LANES:(half + 1) * LANES]


def _store_k(k_ref, h, w_ref, col, width, low, rope):
    for grp, p in _lane_groups(h, w_ref, col, width):
        first, second = _duplicate_heads(rope(p), low)
        k_ref[:, 2 * grp * LANES:(2 * grp + 1) * LANES] = first.astype(k_ref.dtype)
        k_ref[:, (2 * grp + 1) * LANES:(2 * grp + 2) * LANES] = second.astype(k_ref.dtype)


def _store_vt(vt_ref, h, wvt_ref):
    vt = _dot_nt(wvt_ref[...], h)
    ones = jnp.ones((VT_ROWS - HEAD_DIM, Q_BLOCK), vt_ref.dtype)
    for t in range(vt_ref.shape[0]):
        for j in range(N_KV_HEADS):
            vt_ref[t, j * VT_ROWS:j * VT_ROWS + HEAD_DIM, :] = (
                vt[j * HEAD_DIM:(j + 1) * HEAD_DIM, t * Q_BLOCK:(t + 1) * Q_BLOCK].astype(vt_ref.dtype))
            vt_ref[t, j * VT_ROWS + HEAD_DIM:(j + 1) * VT_ROWS, :] = ones


def _inproj_kernel(x_ref, mod_ref, g_ref, w_ref, wvt_ref, cos_ref, sin_ref,
                   q_ref, k_ref, vt_ref, u_ref, ga_ref, gf_ref, *, offsets):
    h = _adaln(x_ref[...], g_ref[...], mod_ref, 3).astype(BF16)
    cos = cos_ref[...]
    sin = sin_ref[...]
    lane = _lane_iota(h.shape[0])
    even_lane = (lane % 2) == 0
    low = lane < HEAD_DIM
    q_scale = HEAD_DIM ** -0.5
    rope = functools.partial(_rope, cos=cos, sin=sin, even_lane=even_lane)
    q_off, k_off, u_off, ga_off, gf_off = offsets

    for grp, p in _lane_groups(h, w_ref, q_off, q_ref.shape[1]):
        q_ref[:, grp * LANES:(grp + 1) * LANES] = (rope(p) * q_scale).astype(q_ref.dtype)
    _store_k(k_ref, h, w_ref, k_off, k_ref.shape[1] // 2, low, rope)
    _store_vt(vt_ref, h, wvt_ref)
    for grp, p in _lane_groups(h, w_ref, u_off, u_ref.shape[1]):
        u_ref[:, grp * LANES:(grp + 1) * LANES] = p.astype(u_ref.dtype)
    for ref, off in ((ga_ref, ga_off), (gf_ref, gf_off)):
        for grp, p in _lane_groups(h, w_ref, off, ref.shape[1]):
            ref[:, grp * LANES:(grp + 1) * LANES] = jax.nn.sigmoid(p).astype(ref.dtype)


def _in_projection(x, mod, g, w, w_vt, cos_t, sin_t, fourier_width, *, rows):
    b, n, d = x.shape
    q_w, kv_w = N_HEADS * HEAD_DIM, N_KV_HEADS * HEAD_DIM
    assert n % rows == 0 and rows % Q_BLOCK == 0 and kv_w % MXU_DIM == 0 and fourier_width % MXU_DIM == 0
    u_off = q_w + 2 * kv_w
    offsets = (0, q_w, u_off, u_off + fourier_width, u_off + fourier_width + d)
    kern = functools.partial(_inproj_kernel, offsets=offsets)
    tok = lambda width: pl.BlockSpec((None, rows, width), lambda si, bi: (bi, si, 0))
    key_blocks = rows // Q_BLOCK
    vt_rows = N_KV_HEADS * VT_ROWS
    out_widths = (q_w, 2 * kv_w, None, fourier_width, d, d)
    out_specs = [tok(width) if width else
                 pl.BlockSpec((None, key_blocks, vt_rows, Q_BLOCK), lambda si, bi: (bi, si, 0, 0))
                 for width in out_widths]
    out_shape = [jax.ShapeDtypeStruct((b, n, width), BF16) if width else
                 jax.ShapeDtypeStruct((b, n // Q_BLOCK, vt_rows, Q_BLOCK), BF16)
                 for width in out_widths]
    return pl.pallas_call(
        kern,
        grid=(n // rows, b),
        in_specs=[
            tok(d),
            pl.BlockSpec((None, N_MOD, d), lambda si, bi: (bi, 0, 0)),
            _resident((1, d)),
            _resident(w.shape),
            _resident(w_vt.shape),
            pl.BlockSpec((rows, LANES), lambda si, bi: (si, 0)),
            pl.BlockSpec((rows, LANES), lambda si, bi: (si, 0)),
        ],
        out_specs=out_specs,
        out_shape=out_shape,
        compiler_params=_params(2),
        name="in_projection",
    )(x, mod, g.reshape(1, d), w, w_vt, cos_t, sin_t)


def _ctx_kv_kernel(x_ref, mod_ref, g_ref, w_ref, wvt_ref, k_ref, vt_ref, *, k_off):
    h = _adaln(x_ref[...], g_ref[...], mod_ref, 3).astype(BF16)
    low = _lane_iota(h.shape[0]) < HEAD_DIM
    _store_k(k_ref, h, w_ref, k_off, k_ref.shape[1] // 2, low, lambda p: p)
    _store_vt(vt_ref, h, wvt_ref)


def _ctx_kv(ctx, mod_ctx, g, w, w_vt):
    b, n, d = ctx.shape
    kv_w = N_KV_HEADS * HEAD_DIM
    vt_rows = N_KV_HEADS * VT_ROWS
    assert n % Q_BLOCK == 0
    kern = functools.partial(_ctx_kv_kernel, k_off=N_HEADS * HEAD_DIM)
    return pl.pallas_call(
        kern,
        grid=(b,),
        in_specs=[pl.BlockSpec((None, n, d), lambda bi: (bi, 0, 0)),
                  pl.BlockSpec((None, N_MOD, d), lambda bi: (0, 0, 0)),
                  _resident((1, d)), _resident(w.shape), _resident(w_vt.shape)],
        out_specs=[pl.BlockSpec((None, n, 2 * kv_w), lambda bi: (bi, 0, 0)),
                   pl.BlockSpec((None, n // Q_BLOCK, vt_rows, Q_BLOCK), lambda bi: (bi, 0, 0, 0))],
        out_shape=[jax.ShapeDtypeStruct((b, n, 2 * kv_w), BF16),
                   jax.ShapeDtypeStruct((b, n // Q_BLOCK, vt_rows, Q_BLOCK), BF16)],
        compiler_params=_params(1),
        name="ctx_kv",
    )(ctx, mod_ctx, g.reshape(1, d), w, w_vt)


def _attn_kernel(sink_ref, q_ref, k_ref, vt_ref, kc_ref, vct_ref, o_ref, *, seq):
    tile_start = pl.program_id(1) * ATTN_ROWS
    lane = _lane_iota(Q_BLOCK)
    keep_low = (lane < HEAD_DIM).astype(BF16)
    keep_high = 1.0 - keep_low
    key_row = lax.broadcasted_iota(jnp.int32, (KEY_SPAN, Q_BLOCK), 0)
    query_col = lax.broadcasted_iota(jnp.int32, (KEY_SPAN, Q_BLOCK), 1)
    head_of_lane = lax.broadcasted_iota(jnp.int32, (1, Q_GROUP * Q_BLOCK), 1) // Q_BLOCK
    n_ctx_blocks = vct_ref.shape[0]

    def window(sb):
        q_start = tile_start + sb * Q_BLOCK
        k_start = pl.multiple_of(jnp.clip(q_start - Q_BLOCK, 0, seq - KEY_SPAN), Q_BLOCK)
        dist = key_row - query_col + (k_start - q_start)
        bias = jnp.where(jnp.abs(dist) <= WINDOW, 0.0, NEG_INF).astype(F32)
        return pl.multiple_of(sb * Q_BLOCK, Q_BLOCK), k_start, bias

    def scores(win, j):
        r0, k_start, bias = win
        ks = slice(j * LANES, (j + 1) * LANES)
        qa = q_ref[pl.ds(r0, Q_BLOCK), 2 * j * LANES:(2 * j + 1) * LANES]
        qb = q_ref[pl.ds(r0, Q_BLOCK), (2 * j + 1) * LANES:(2 * j + 2) * LANES]
        q4 = jnp.concatenate([qa * keep_low, qa * keep_high, qb * keep_low, qb * keep_high], axis=0)
        keys = jnp.concatenate([k_ref[pl.ds(k_start, KEY_SPAN), ks], kc_ref[:, ks]], axis=0)
        s_all = _dot_nt(keys, q4)
        s_loc = jnp.concatenate([s_all[:KEY_SPAN, g * Q_BLOCK:(g + 1) * Q_BLOCK] + bias for g in range(Q_GROUP)],
                                axis=1)
        return s_loc.astype(BF16), s_all[KEY_SPAN:].astype(BF16)

    def block(step, carry):
        wins = [window(step * ATTN_UNROLL + i) for i in range(ATTN_UNROLL)]
        units = [(win, j) for win in wins for j in range(N_KV_HEADS)]
        pending = scores(*units[0])
        for idx, (win, j) in enumerate(units):
            r0, k_start, _ = win
            k_block = k_start // Q_BLOCK
            s_loc, s_ctx = pending
            if idx + 1 < len(units):
                pending = scores(*units[idx + 1])
            vs = slice(j * VT_ROWS, (j + 1) * VT_ROWS)
            sink = jnp.zeros((1, Q_GROUP * Q_BLOCK), F32)
            for gi in range(Q_GROUP):
                sink = jnp.where(head_of_lane == gi, sink_ref[j * Q_GROUP + gi], sink)
            top = jnp.maximum(jnp.max(s_loc, axis=0, keepdims=True), jnp.max(s_ctx, axis=0, keepdims=True))
            m_b = jnp.maximum(top.astype(F32), sink).astype(BF16)
            m = m_b.astype(F32)
            p_all = jnp.concatenate([jnp.exp(s_loc - m_b), jnp.exp(s_ctx - m_b)], axis=0)
            vt_all = jnp.concatenate([vt_ref[k_block + t, vs, :] for t in range(KEY_BLOCKS)]
                                     + [vct_ref[t, vs, :] for t in range(n_ctx_blocks)], axis=1)
            out_t = _dot(vt_all, p_all)
            denom = out_t[HEAD_DIM:HEAD_DIM + 1, :] + jnp.exp(sink - m)
            vals = out_t[:HEAD_DIM, :] / denom
            for pair in range(Q_GROUP // 2):
                two_heads = jnp.concatenate([vals[:, (2 * pair) * Q_BLOCK:(2 * pair + 1) * Q_BLOCK],
                                             vals[:, (2 * pair + 1) * Q_BLOCK:(2 * pair + 2) * Q_BLOCK]], axis=0)
                o_ref[pl.ds(r0, Q_BLOCK), (2 * j + pair) * LANES:(2 * j + pair + 1) * LANES] = (
                    two_heads.T.astype(o_ref.dtype))
        return carry

    lax.fori_loop(0, ATTN_ROWS // (Q_BLOCK * ATTN_UNROLL), block, 0)


def _attention(sink, q, k2, vt, kc2, vct):
    b, n, dq = q.shape
    n_ctx, dk = kc2.shape[1], k2.shape[2]
    assert n % ATTN_ROWS == 0 and n >= KEY_SPAN and n_ctx % Q_BLOCK == 0
    kern = functools.partial(_attn_kernel, seq=n)
    per_sample = lambda arr: pl.BlockSpec((None,) + arr.shape[1:], lambda bi, si: (bi,) + (0,) * (arr.ndim - 1))
    return pl.pallas_call(
        kern,
        grid=(b, n // ATTN_ROWS),
        in_specs=[
            pl.BlockSpec(memory_space=pltpu.SMEM),
            pl.BlockSpec((None, ATTN_ROWS, dq), lambda bi, si: (bi, si, 0)),
            per_sample(k2), per_sample(vt), per_sample(kc2), per_sample(vct),
        ],
        out_specs=pl.BlockSpec((None, ATTN_ROWS, dq), lambda bi, si: (bi, si, 0)),
        out_shape=jax.ShapeDtypeStruct((b, n, dq), BF16),
        compiler_params=_params(2),
        name="attention",
    )(sink, q, k2, vt, kc2, vct)


def _fourier_kernel(u_ref, tc_ref, ts_ref, wch_ref, wseq_ref, o_ref, v_scr, y_scr, *, n2):
    gd = FOURIER_GROUP_DIM
    width = N_FOURIER_GROUPS * gd
    row_chunk = 256

    def radix_stage(k1):
        slot = k1 % 2
        for g in range(N_FOURIER_GROUPS):
            for r0 in range(0, n2, row_chunk):
                vr = jnp.zeros((row_chunk, gd), F32)
                vi = jnp.zeros((row_chunk, gd), F32)
                for n1 in range(FFT_RADIX):
                    rs = slice(n1 * n2 + r0, n1 * n2 + r0 + row_chunk)
                    ug = u_ref[rs, g * gd:(g + 1) * gd].astype(F32)
                    vr = vr + ug * tc_ref[k1, rs, :]
                    vi = vi - ug * ts_ref[k1, rs, :]
                v_scr[slot, r0:r0 + row_chunk, 2 * g * gd:(2 * g + 1) * gd] = vr.astype(BF16)
                v_scr[slot, r0:r0 + row_chunk, (2 * g + 1) * gd:(2 * g + 2) * gd] = vi.astype(BF16)

    def channel_stage(k1):
        slot = k1 % 2
        for g in range(N_FOURIER_GROUPS):
            y = _dot(v_scr[slot, :, 2 * g * gd:(2 * g + 2) * gd], wch_ref[...])
            y_scr[slot, 0:n2, g * gd:(g + 1) * gd] = y[:, :gd].astype(BF16)
            y_scr[slot, n2:2 * n2, g * gd:(g + 1) * gd] = y[:, gd:].astype(BF16)

    def sequence_stage(k1):
        o_ref[:, k1 * width:(k1 + 1) * width] = _dot(wseq_ref[...], y_scr[k1 % 2]).astype(o_ref.dtype)

    radix_stage(0)
    for k1 in range(FFT_RADIX):
        channel_stage(k1)
        if k1 + 1 < FFT_RADIX:
            radix_stage(k1 + 1)
        sequence_stage(k1)


def _fourier_tables(n):
    n2 = n // FFT_RADIX
    gd = FOURIER_GROUP_DIM
    pos = np.arange(n, dtype=np.int64)
    k1 = np.arange(FFT_RADIX, dtype=np.int64)
    ang = 2.0 * np.pi * ((k1[:, None] * pos[None, :]) % n) / n
    tw_cos = np.broadcast_to(np.cos(ang)[:, :, None], (FFT_RADIX, n, LANES)).astype(np.float32)
    tw_sin = np.broadcast_to(np.sin(ang)[:, :, None], (FFT_RADIX, n, LANES)).astype(np.float32)
    ch = np.arange(gd, dtype=np.int64)
    ang_c = 2.0 * np.pi * ((ch[:, None] * ch[None, :]) % gd) / gd
    cc, sc = np.cos(ang_c), np.sin(ang_c)
    w_ch = np.block([[cc, -sc], [sc, cc]]).astype(np.float32)
    kk = np.arange(n2, dtype=np.int64)
    ang_s = 2.0 * np.pi * ((kk[:, None] * kk[None, :]) % n2) / n2
    scale = 1.0 / np.sqrt(float(n) * gd)
    w_seq = (np.concatenate([np.cos(ang_s), np.sin(ang_s)], axis=1) * scale).astype(np.float32)
    return jnp.asarray(tw_cos), jnp.asarray(tw_sin), jnp.asarray(w_ch), jnp.asarray(w_seq)


def _fourier(u):
    b, n, width = u.shape
    assert n % (FFT_RADIX * 256) == 0
    n2 = n // FFT_RADIX
    tw_cos, tw_sin, w_ch, w_seq = _fourier_tables(n)
    kern = functools.partial(_fourier_kernel, n2=n2)
    return pl.pallas_call(
        kern,
        grid=(b,),
        in_specs=[
            pl.BlockSpec((None, n, width), lambda bi: (bi, 0, 0)),
            _resident(tw_cos.shape),
            _resident(tw_sin.shape),
            _resident(w_ch.shape),
            _resident(w_seq.shape),
        ],
        out_specs=pl.BlockSpec((None, n2, FFT_RADIX * width), lambda bi: (bi, 0, 0)),
        out_shape=jax.ShapeDtypeStruct((b, n2, FFT_RADIX * width), BF16),
        scratch_shapes=[pltpu.VMEM((2, n2, 2 * width), BF16), pltpu.VMEM((2, 2 * n2, width), BF16)],
        compiler_params=_params(1),
        name="fourier",
    )(u, tw_cos, tw_sin, w_ch.astype(BF16), w_seq.astype(BF16))


def _merge_kernel(x_ref, mod_ref, a_ref, f_ref, ga_ref, gf_ref, wa_ref, wf_ref, wo_ref, o_ref, fw_scr):
    quarter, width = f_ref.shape[0], wf_ref.shape[0]
    groups = fw_scr.shape[0]
    for k1 in range(FFT_RADIX):
        fw = _dot(f_ref[:, k1 * width:(k1 + 1) * width], wf_ref[...])
        for grp in range(groups):
            fw_scr[grp, pl.ds(k1, quarter, stride=FFT_RADIX), :] = fw[:, grp * LANES:(grp + 1) * LANES]
    fw = jnp.concatenate([fw_scr[grp] for grp in range(groups)], axis=-1)
    y = ga_ref[...].astype(F32) * _dot(a_ref[...], wa_ref[...]) + gf_ref[...].astype(F32) * fw
    o_ref[...] = x_ref[...] + mod_ref[5:6, :] * _dot(y.astype(BF16), wo_ref[...])


def _merge(x, mod, a, f, ga, gf, w_a, w_f, w_o, *, rows):
    b, n, d = x.shape
    assert rows % FFT_RADIX == 0
    tok = lambda width: pl.BlockSpec((None, rows, width), lambda bi, si: (bi, si, 0))
    return pl.pallas_call(
        _merge_kernel,
        grid=(b, n // rows),
        in_specs=[
            tok(d),
            pl.BlockSpec((None, N_MOD, d), lambda bi, si: (bi, 0, 0)),
            tok(a.shape[2]),
            pl.BlockSpec((None, rows // FFT_RADIX, f.shape[2]), lambda bi, si: (bi, si, 0)),
            tok(d), tok(d),
            _resident(w_a.shape), _resident(w_f.shape), _resident(w_o.shape),
        ],
        out_specs=tok(d),
        out_shape=jax.ShapeDtypeStruct((b, n, d), F32),
        scratch_shapes=[pltpu.VMEM((d // LANES, rows, LANES), F32)],
        compiler_params=_params(2),
        name="merge",
    )(x, mod, a, f, ga, gf, w_a, w_f, w_o)


def _rope_tables(n):
    rows = n // GRID_W
    half = HEAD_DIM // 2
    row = jnp.repeat(jnp.arange(rows, dtype=F32), GRID_W)
    col = jnp.tile(jnp.arange(GRID_W, dtype=F32), rows)
    inv_freq = ROPE_BASE ** (-jnp.arange(0, half, 2, dtype=F32) / half)
    ang = jnp.concatenate([row[:, None] * inv_freq, col[:, None] * inv_freq], axis=-1)
    cos, sin = jnp.cos(ang), jnp.sin(ang)
    reps = LANES // HEAD_DIM
    cos_t = jnp.tile(jnp.repeat(cos, 2, axis=-1), (1, reps))
    sin_t = jnp.tile(jnp.stack([-sin, sin], axis=-1).reshape(n, HEAD_DIM), (1, reps))
    return cos_t, sin_t


def kernel(x, c, ctx, c_ctx, w_ada, b_ada, g_ffn1, w_ffn1_in, w_ffn1_out, g_mix, w_in, attn_sink,
           w_attn_branch, w_fourier_branch, w_out, g_ffn2, w_ffn2_in, w_ffn2_out, g_final):
    b, n, d = x.shape
    depth = w_ada.shape[0]
    assert depth == 1, "single-layer problem: the last-layer path of the reference"
    layer = 0
    fourier_width = N_FOURIER_GROUPS * FOURIER_GROUP_DIM

    cond_rows = 8 * ((b + 1 + 7) // 8)
    cond_in = jnp.zeros((cond_rows, d), F32).at[:b].set(c).at[b].set(c_ctx)
    mod = _modulation(cond_in, w_ada[layer], b_ada[layer]).reshape(cond_rows, N_MOD, d)
    mod_lat, mod_ctx = mod[:b], mod[b:b + 1]

    bf = lambda w: w.astype(BF16)
    w_proj = bf(w_in[layer])
    v_off = (N_HEADS + N_KV_HEADS) * HEAD_DIM
    w_vt = w_proj[:, v_off:v_off + N_KV_HEADS * HEAD_DIM].T
    cos_t, sin_t = _rope_tables(n)

    w1_in, w1_out = bf(w_ffn1_in[layer]), bf(w_ffn1_out[layer])
    x1 = _ffn(x, mod_lat, g_ffn1[layer], w1_in, w1_out, g_final, base=0, final_norm=False, rows=TOKEN_ROWS)
    ctx_flat = ctx.reshape(1, ctx.shape[0] * ctx.shape[1], d)
    ctx_rows = TOKEN_ROWS if ctx_flat.shape[1] % TOKEN_ROWS == 0 else ctx.shape[1]
    ctx1 = _ffn(ctx_flat, mod_ctx, g_ffn1[layer], w1_in, w1_out, g_final, base=0, final_norm=False,
                rows=ctx_rows).reshape(ctx.shape)

    q, k2, vt, u, ga, gf = _in_projection(x1, mod_lat, g_mix[layer], w_proj, w_vt, cos_t, sin_t, fourier_width,
                                          rows=TOKEN_ROWS)
    kc2, vct = _ctx_kv(ctx1, mod_ctx, g_mix[layer], w_proj, w_vt)

    a = _attention(attn_sink[layer], q, k2, vt, kc2, vct)
    f = _fourier(u)

    x2 = _merge(x1, mod_lat, a, f, ga, gf, bf(w_attn_branch[layer]), bf(w_fourier_branch[layer]),
                bf(w_out[layer]), rows=TOKEN_ROWS)
    return _ffn(x2, mod_lat, g_ffn2[layer], bf(w_ffn2_in[layer]), bf(w_ffn2_out[layer]), g_final,
                base=6, final_norm=True, rows=TOKEN_ROWS)
```

```python
import functools

import numpy as np
import jax
import jax.numpy as jnp
from jax import lax
from jax.experimental import pallas as pl
from jax.experimental.pallas import tpu as pltpu

N_HEADS = 16
N_KV_HEADS = 4
HEAD_DIM = 64
Q_GROUP = N_HEADS // N_KV_HEADS
WINDOW = 128
GRID_W = 64
N_FOURIER_GROUPS = 4
FOURIER_GROUP_DIM = 128
ROPE_BASE = 10000.0
N_MOD = 9
RMS_EPS = 1e-6
NEG_INF = -1e30

LANES = 128
MXU_DIM = 256
BF16_SUBLANES = 16
VMEM_LIMIT_BYTES = 56 * 1024 * 1024

TOKEN_ROWS = 1024
MIX_ROWS = 512
FF_CHUNK = MXU_DIM
ATTN_ROWS = 1024
Q_BLOCK = 128
ATTN_UNROLL = 8
SCORE_LOOKAHEAD = 2
KEY_BLOCKS = 3
KEY_SPAN = KEY_BLOCKS * Q_BLOCK
VT_ROWS = HEAD_DIM + BF16_SUBLANES
FFT_RADIX = 4

BF16 = jnp.bfloat16
F32 = jnp.float32


def _params(grid_rank):
    return pltpu.CompilerParams(dimension_semantics=("arbitrary",) * grid_rank, vmem_limit_bytes=VMEM_LIMIT_BYTES)


def _dot(a, b):
    return jnp.dot(a, b, preferred_element_type=F32)


def _dot_nt(a, b):
    return lax.dot_general(a, b, (((1,), (1,)), ((), ())), preferred_element_type=F32)


def _rms(x, g):
    return x * lax.rsqrt(jnp.mean(x * x, axis=-1, keepdims=True) + RMS_EPS) * g


def _adaln(x, g, mod_ref, base):
    shift = mod_ref[base:base + 1, :]
    scale = mod_ref[base + 1:base + 2, :]
    return _rms(x, g) * (1.0 + scale) + shift


def _resident(shape):
    zeros = (0,) * len(shape)
    return pl.BlockSpec(shape, lambda *_: zeros)


def _lane_iota(rows):
    return lax.broadcasted_iota(jnp.int32, (rows, LANES), 1)


def _mod_kernel(c_ref, w_ref, b_ref, o_ref):
    c = c_ref[...]
    cond = (c * jax.nn.sigmoid(c)).astype(BF16)
    o_ref[...] = _dot(cond, w_ref[...].astype(BF16)) + b_ref[...]


def _modulation(cond_in, w_ada, b_ada):
    rows, d = cond_in.shape
    n_out = w_ada.shape[1]
    return pl.pallas_call(
        _mod_kernel,
        grid=(n_out // d,),
        in_specs=[
            _resident((rows, d)),
            pl.BlockSpec((d, d), lambda j: (0, j)),
            pl.BlockSpec((1, d), lambda j: (0, j)),
        ],
        out_specs=pl.BlockSpec((rows, d), lambda j: (0, j)),
        out_shape=jax.ShapeDtypeStruct((rows, n_out), F32),
        compiler_params=_params(1),
        name="modulation",
    )(cond_in, w_ada, b_ada.reshape(1, n_out))


def _macaron_ffn(x, mod_ref, g_ref, win_ref, wout_ref, base):
    d_ff = wout_ref.shape[0]
    h = _adaln(x, g_ref[...], mod_ref, base).astype(BF16)
    acc = jnp.zeros(x.shape, F32)
    for c0 in range(0, d_ff, FF_CHUNK):
        a = _dot(h, win_ref[:, c0:c0 + FF_CHUNK])
        u = _dot(h, win_ref[:, d_ff + c0:d_ff + c0 + FF_CHUNK])
        act = (a * jax.nn.sigmoid(a) * u).astype(BF16)
        acc = acc + _dot(act, wout_ref[c0:c0 + FF_CHUNK, :])
    return x + 0.5 * mod_ref[base + 2:base + 3, :] * acc


def _ffn_kernel(x_ref, mod_ref, g_ref, win_ref, wout_ref, o_ref, *, base):
    o_ref[...] = _macaron_ffn(x_ref[...], mod_ref, g_ref, win_ref, wout_ref, base)


def _mix_ffn_kernel(x_ref, mod_ref, ya_ref, fw_ref, gf_ref, wo_ref, g_ref, win_ref, wout_ref, gfin_ref, o_ref,
                    fw_scr, *, base):
    quarter, d = fw_ref.shape[0], x_ref.shape[1]
    groups = fw_scr.shape[0]
    for k1 in range(FFT_RADIX):
        for grp in range(groups):
            cols = slice(k1 * d + grp * LANES, k1 * d + (grp + 1) * LANES)
            fw_scr[grp, pl.ds(k1, quarter, stride=FFT_RADIX), :] = fw_ref[:, cols].astype(F32)
    fw = jnp.concatenate([fw_scr[grp] for grp in range(groups)], axis=-1)
    y = ya_ref[...].astype(F32) + gf_ref[...].astype(F32) * fw
    x = x_ref[...] + mod_ref[5:6, :] * _dot(y.astype(BF16), wo_ref[...])
    out = _macaron_ffn(x, mod_ref, g_ref, win_ref, wout_ref, base)
    o_ref[...] = _rms(out, gfin_ref[...])


def _ffn(x, mod, g, w_in, w_out, *, base, rows):
    b, n, d = x.shape
    d_ff = w_out.shape[0]
    assert n % rows == 0 and d_ff % FF_CHUNK == 0
    per_batch_mod = mod.shape[0] == b
    mod_index = (lambda bi, si: (bi, 0, 0)) if per_batch_mod else (lambda bi, si: (0, 0, 0))
    return pl.pallas_call(
        functools.partial(_ffn_kernel, base=base),
        grid=(b, n // rows),
        in_specs=[
            pl.BlockSpec((None, rows, d), lambda bi, si: (bi, si, 0)),
            pl.BlockSpec((None, N_MOD, d), mod_index),
            _resident((1, d)),
            _resident((d, 2 * d_ff)),
            _resident((d_ff, d)),
        ],
        out_specs=pl.BlockSpec((None, rows, d), lambda bi, si: (bi, si, 0)),
        out_shape=jax.ShapeDtypeStruct((b, n, d), F32),
        compiler_params=_params(2),
        name="ffn",
    )(x, mod, g.reshape(1, d), w_in, w_out)


def _mix_ffn(x, mod, ya, fw, gf, w_o, g, w_in, w_out, g_final, *, base, rows):
    b, n, d = x.shape
    d_ff = w_out.shape[0]
    assert n % rows == 0 and rows % FFT_RADIX == 0 and d_ff % FF_CHUNK == 0 and fw.shape[2] == FFT_RADIX * d
    tok = lambda width: pl.BlockSpec((None, rows, width), lambda bi, si: (bi, si, 0))
    return pl.pallas_call(
        functools.partial(_mix_ffn_kernel, base=base),
        grid=(b, n // rows),
        in_specs=[
            tok(d),
            pl.BlockSpec((None, N_MOD, d), lambda bi, si: (bi, 0, 0)),
            tok(d),
            pl.BlockSpec((None, rows // FFT_RADIX, FFT_RADIX * d), lambda bi, si: (bi, si, 0)),
            tok(d),
            _resident(w_o.shape),
            _resident((1, d)),
            _resident((d, 2 * d_ff)),
            _resident((d_ff, d)),
            _resident((1, d)),
        ],
        out_specs=tok(d),
        out_shape=jax.ShapeDtypeStruct((b, n, d), F32),
        scratch_shapes=[pltpu.VMEM((d // LANES, rows, LANES), F32)],
        compiler_params=_params(2),
        name="mix_ffn_final",
    )(x, mod, ya, fw, gf, w_o, g.reshape(1, d), w_in, w_out, g_final.reshape(1, d))


def _rope(p, cos, sin, even_lane):
    partner = jnp.where(even_lane, pltpu.roll(p, LANES - 1, 1), pltpu.roll(p, 1, 1))
    return p * cos + partner * sin


def _duplicate_heads(pair, low):
    swapped = pltpu.roll(pair, HEAD_DIM, 1)
    return jnp.where(low, pair, swapped), jnp.where(low, swapped, pair)


def _lane_groups(h, w_ref, col, width):
    for c0 in range(0, width, MXU_DIM):
        p = _dot(h, w_ref[:, col + c0:col + c0 + MXU_DIM])
        for half in range(MXU_DIM // LANES):
            yield c0 // LANES + half, p[:, half * LANES:(half + 1) * LANES]


def _store_k(k_ref, h, w_ref, col, width, low, rope):
    for grp, p in _lane_groups(h, w_ref, col, width):
        first, second = _duplicate_heads(rope(p), low)
        k_ref[:, 2 * grp * LANES:(2 * grp + 1) * LANES] = first.astype(k_ref.dtype)
        k_ref[:, (2 * grp + 1) * LANES:(2 * grp + 2) * LANES] = second.astype(k_ref.dtype)


def _store_vt(vt_ref, h, wvt_ref):
    vt = _dot_nt(wvt_ref[...], h)
    ones = jnp.ones((VT_ROWS - HEAD_DIM, Q_BLOCK), vt_ref.dtype)
    for t in range(vt_ref.shape[0]):
        for j in range(N_KV_HEADS):
            vt_ref[t, j * VT_ROWS:j * VT_ROWS + HEAD_DIM, :] = (
                vt[j * HEAD_DIM:(j + 1) * HEAD_DIM, t * Q_BLOCK:(t + 1) * Q_BLOCK].astype(vt_ref.dtype))
            vt_ref[t, j * VT_ROWS + HEAD_DIM:(j + 1) * VT_ROWS, :] = ones


def _inproj_kernel(x_ref, mod_ref, g_ref, w_ref, wvt_ref, cos_ref, sin_ref,
                   q_ref, k_ref, vt_ref, u_ref, ga_ref, gf_ref, *, offsets):
    h = _adaln(x_ref[...], g_ref[...], mod_ref, 3).astype(BF16)
    cos = cos_ref[...]
    sin = sin_ref[...]
    lane = _lane_iota(h.shape[0])
    even_lane = (lane % 2) == 0
    low = lane < HEAD_DIM
    q_scale = HEAD_DIM ** -0.5
    rope = functools.partial(_rope, cos=cos, sin=sin, even_lane=even_lane)
    q_off, k_off, u_off, ga_off, gf_off = offsets

    for grp, p in _lane_groups(h, w_ref, q_off, q_ref.shape[1]):
        q_ref[:, grp * LANES:(grp + 1) * LANES] = (rope(p) * q_scale).astype(q_ref.dtype)
    _store_k(k_ref, h, w_ref, k_off, k_ref.shape[1] // 2, low, rope)
    _store_vt(vt_ref, h, wvt_ref)
    for grp, p in _lane_groups(h, w_ref, u_off, u_ref.shape[1]):
        u_ref[:, grp * LANES:(grp + 1) * LANES] = p.astype(u_ref.dtype)
    for ref, off in ((ga_ref, ga_off), (gf_ref, gf_off)):
        for grp, p in _lane_groups(h, w_ref, off, ref.shape[1]):
            ref[:, grp * LANES:(grp + 1) * LANES] = jax.nn.sigmoid(p).astype(ref.dtype)


def _in_projection(x, mod, g, w, w_vt, cos_t, sin_t, fourier_width, *, rows):
    b, n, d = x.shape
    q_w, kv_w = N_HEADS * HEAD_DIM, N_KV_HEADS * HEAD_DIM
    assert n % rows == 0 and rows % Q_BLOCK == 0 and kv_w % MXU_DIM == 0 and fourier_width % MXU_DIM == 0
    u_off = q_w + 2 * kv_w
    offsets = (0, q_w, u_off, u_off + fourier_width, u_off + fourier_width + d)
    kern = functools.partial(_inproj_kernel, offsets=offsets)
    tok = lambda width: pl.BlockSpec((None, rows, width), lambda si, bi: (bi, si, 0))
    key_blocks = rows // Q_BLOCK
    vt_rows = N_KV_HEADS * VT_ROWS
    out_widths = (q_w, 2 * kv_w, None, fourier_width, d, d)
    out_specs = [tok(width) if width else
                 pl.BlockSpec((None, key_blocks, vt_rows, Q_BLOCK), lambda si, bi: (bi, si, 0, 0))
                 for width in out_widths]
    out_shape = [jax.ShapeDtypeStruct((b, n, width), BF16) if width else
                 jax.ShapeDtypeStruct((b, n // Q_BLOCK, vt_rows, Q_BLOCK), BF16)
                 for width in out_widths]
    return pl.pallas_call(
        kern,
        grid=(n // rows, b),
        in_specs=[
            tok(d),
            pl.BlockSpec((None, N_MOD, d), lambda si, bi: (bi, 0, 0)),
            _resident((1, d)),
            _resident(w.shape),
            _resident(w_vt.shape),
            pl.BlockSpec((rows, LANES), lambda si, bi: (si, 0)),
            pl.BlockSpec((rows, LANES), lambda si, bi: (si, 0)),
        ],
        out_specs=out_specs,
        out_shape=out_shape,
        compiler_params=_params(2),
        name="in_projection",
    )(x, mod, g.reshape(1, d), w, w_vt, cos_t, sin_t)


def _ctx_kv_kernel(x_ref, mod_ref, g_ref, w_ref, wvt_ref, k_ref, vt_ref, *, k_off):
    h = _adaln(x_ref[...], g_ref[...], mod_ref, 3).astype(BF16)
    low = _lane_iota(h.shape[0]) < HEAD_DIM
    _store_k(k_ref, h, w_ref, k_off, k_ref.shape[1] // 2, low, lambda p: p)
    _store_vt(vt_ref, h, wvt_ref)


def _ctx_kv(ctx, mod_ctx, g, w, w_vt):
    b, n, d = ctx.shape
    kv_w = N_KV_HEADS * HEAD_DIM
    vt_rows = N_KV_HEADS * VT_ROWS
    assert n % Q_BLOCK == 0
    kern = functools.partial(_ctx_kv_kernel, k_off=N_HEADS * HEAD_DIM)
    return pl.pallas_call(
        kern,
        grid=(b,),
        in_specs=[pl.BlockSpec((None, n, d), lambda bi: (bi, 0, 0)),
                  pl.BlockSpec((None, N_MOD, d), lambda bi: (0, 0, 0)),
                  _resident((1, d)), _resident(w.shape), _resident(w_vt.shape)],
        out_specs=[pl.BlockSpec((None, n, 2 * kv_w), lambda bi: (bi, 0, 0)),
                   pl.BlockSpec((None, n // Q_BLOCK, vt_rows, Q_BLOCK), lambda bi: (bi, 0, 0, 0))],
        out_shape=[jax.ShapeDtypeStruct((b, n, 2 * kv_w), BF16),
                   jax.ShapeDtypeStruct((b, n // Q_BLOCK, vt_rows, Q_BLOCK), BF16)],
        compiler_params=_params(1),
        name="ctx_kv",
    )(ctx, mod_ctx, g.reshape(1, d), w, w_vt)


def _attn_kernel(sink_ref, q_ref, k_ref, vt_ref, kc_ref, vct_ref, wa_ref, ga_ref, o_ref, *, seq):
    tile_start = pl.program_id(1) * ATTN_ROWS
    lane = _lane_iota(Q_BLOCK)
    keep_low = (lane < HEAD_DIM).astype(BF16)
    keep_high = 1.0 - keep_low
    key_row = lax.broadcasted_iota(jnp.int32, (KEY_SPAN, Q_BLOCK), 0)
    query_col = lax.broadcasted_iota(jnp.int32, (KEY_SPAN, Q_BLOCK), 1)
    head_of_lane = lax.broadcasted_iota(jnp.int32, (1, Q_GROUP * Q_BLOCK), 1) // Q_BLOCK
    n_ctx_blocks = vct_ref.shape[0]

    def window(sb):
        q_start = tile_start + sb * Q_BLOCK
        k_start = pl.multiple_of(jnp.clip(q_start - Q_BLOCK, 0, seq - KEY_SPAN), Q_BLOCK)
        dist = key_row - query_col + (k_start - q_start)
        bias = jnp.where(jnp.abs(dist) <= WINDOW, 0.0, NEG_INF).astype(F32)
        return pl.multiple_of(sb * Q_BLOCK, Q_BLOCK), k_start, bias

    def scores(win, j):
        r0, k_start, bias = win
        ks = slice(j * LANES, (j + 1) * LANES)
        qa = q_ref[pl.ds(r0, Q_BLOCK), 2 * j * LANES:(2 * j + 1) * LANES]
        qb = q_ref[pl.ds(r0, Q_BLOCK), (2 * j + 1) * LANES:(2 * j + 2) * LANES]
        q4 = jnp.concatenate([qa * keep_low, qa * keep_high, qb * keep_low, qb * keep_high], axis=0)
        keys = jnp.concatenate([k_ref[pl.ds(k_start, KEY_SPAN), ks], kc_ref[:, ks]], axis=0)
        s_all = _dot_nt(keys, q4)
        s_loc = jnp.concatenate([s_all[:KEY_SPAN, g * Q_BLOCK:(g + 1) * Q_BLOCK] + bias for g in range(Q_GROUP)],
                                axis=1)
        return s_loc.astype(BF16), s_all[KEY_SPAN:].astype(BF16)

    def block(step, carry):
        wins = [window(step * ATTN_UNROLL + i) for i in range(ATTN_UNROLL)]
        units = [(wi, j) for wi in range(ATTN_UNROLL) for j in range(N_KV_HEADS)]
        head_groups = {}
        piece = wa_ref.shape[1] // N_KV_HEADS

        def project(wi, c):
            attn_rows = jnp.concatenate(head_groups[wi], axis=1)
            rows, cols = pl.ds(wins[wi][0], Q_BLOCK), slice(c * piece, (c + 1) * piece)
            gated = ga_ref[rows, cols].astype(F32) * _dot(attn_rows, wa_ref[:, cols])
            o_ref[rows, cols] = gated.astype(o_ref.dtype)

        pending = [scores(wins[wi], j) for wi, j in units[:SCORE_LOOKAHEAD]]
        for idx, (wi, j) in enumerate(units):
            k_block = wins[wi][1] // Q_BLOCK
            s_loc, s_ctx = pending.pop(0)
            if idx + SCORE_LOOKAHEAD < len(units):
                nwi, nj = units[idx + SCORE_LOOKAHEAD]
                pending.append(scores(wins[nwi], nj))
            if wi > 0:
                project(wi - 1, j)
            vs = slice(j * VT_ROWS, (j + 1) * VT_ROWS)
            sink = jnp.zeros((1, Q_GROUP * Q_BLOCK), F32)
            for gi in range(Q_GROUP):
                sink = jnp.where(head_of_lane == gi, sink_ref[j * Q_GROUP + gi], sink)
            top = jnp.maximum(jnp.max(s_loc, axis=0, keepdims=True), jnp.max(s_ctx, axis=0, keepdims=True))
            m_b = jnp.maximum(top.astype(F32), sink).astype(BF16)
            m = m_b.astype(F32)
            p_all = jnp.concatenate([jnp.exp(s_loc - m_b), jnp.exp(s_ctx - m_b)], axis=0)
            vt_all = jnp.concatenate([vt_ref[k_block + t, vs, :] for t in range(KEY_BLOCKS)]
                                     + [vct_ref[t, vs, :] for t in range(n_ctx_blocks)], axis=1)
            out_t = _dot(vt_all, p_all)
            denom = out_t[HEAD_DIM:HEAD_DIM + 1, :] + jnp.exp(sink - m)
            vals = out_t[:HEAD_DIM, :] / denom
            for pair in range(Q_GROUP // 2):
                two_heads = jnp.concatenate([vals[:, (2 * pair) * Q_BLOCK:(2 * pair + 1) * Q_BLOCK],
                                             vals[:, (2 * pair + 1) * Q_BLOCK:(2 * pair + 2) * Q_BLOCK]], axis=0)
                head_groups.setdefault(wi, []).append(two_heads.T.astype(BF16))
        for c in range(N_KV_HEADS):
            project(ATTN_UNROLL - 1, c)
        return carry

    lax.fori_loop(0, ATTN_ROWS // (Q_BLOCK * ATTN_UNROLL), block, 0)


def _attention(sink, q, k2, vt, kc2, vct, w_a, ga):
    b, n, dq = q.shape
    n_ctx, d_out = kc2.shape[1], w_a.shape[1]
    assert n % ATTN_ROWS == 0 and n >= KEY_SPAN and n_ctx % Q_BLOCK == 0 and ATTN_ROWS % (Q_BLOCK * ATTN_UNROLL) == 0
    kern = functools.partial(_attn_kernel, seq=n)
    per_sample = lambda arr: pl.BlockSpec((None,) + arr.shape[1:], lambda bi, si: (bi,) + (0,) * (arr.ndim - 1))
    return pl.pallas_call(
        kern,
        grid=(b, n // ATTN_ROWS),
        in_specs=[
            pl.BlockSpec(memory_space=pltpu.SMEM),
            pl.BlockSpec((None, ATTN_ROWS, dq), lambda bi, si: (bi, si, 0)),
            per_sample(k2), per_sample(vt), per_sample(kc2), per_sample(vct),
            _resident(w_a.shape),
            pl.BlockSpec((None, ATTN_ROWS, d_out), lambda bi, si: (bi, si, 0)),
        ],
        out_specs=pl.BlockSpec((None, ATTN_ROWS, d_out), lambda bi, si: (bi, si, 0)),
        out_shape=jax.ShapeDtypeStruct((b, n, d_out), BF16),
        compiler_params=_params(2),
        name="attention",
    )(sink, q, k2, vt, kc2, vct, w_a, ga)


def _fourier_kernel(u_ref, tc_ref, ts_ref, wch_ref, wseq_ref, wf_ref, o_ref, v_scr, y_scr, *, n2):
    assert FFT_RADIX == 4
    gd = FOURIER_GROUP_DIM
    d_out = wf_ref.shape[1]
    row_chunk = 256

    def radix_stage(k1):
        slot = k1 % 2
        for g in range(N_FOURIER_GROUPS):
            for r0 in range(0, n2, row_chunk):
                u0, u1, u2, u3 = (u_ref[n1 * n2 + r0:n1 * n2 + r0 + row_chunk, g * gd:(g + 1) * gd].astype(F32)
                                  for n1 in range(FFT_RADIX))
                if k1 % 2 == 0:
                    er = (u0 + u2) + (u1 + u3) if k1 == 0 else (u0 + u2) - (u1 + u3)
                    ei = None
                else:
                    er = u0 - u2
                    ei = u3 - u1 if k1 == 1 else u1 - u3
                if k1 == 0:
                    vr, vi = er, jnp.zeros_like(er)
                else:
                    c = tc_ref[k1 - 1, r0:r0 + row_chunk, :]
                    s = ts_ref[k1 - 1, r0:r0 + row_chunk, :]
                    vr, vi = (er * c, -(er * s)) if ei is None else (er * c + ei * s, ei * c - er * s)
                v_scr[slot, r0:r0 + row_chunk, 2 * g * gd:(2 * g + 1) * gd] = vr.astype(BF16)
                v_scr[slot, r0:r0 + row_chunk, (2 * g + 1) * gd:(2 * g + 2) * gd] = vi.astype(BF16)

    def channel_stage(k1):
        slot = k1 % 2
        for g in range(N_FOURIER_GROUPS):
            y = _dot(v_scr[slot, :, 2 * g * gd:(2 * g + 2) * gd], wch_ref[...])
            y_scr[slot, 0:n2, g * gd:(g + 1) * gd] = y[:, :gd].astype(BF16)
            y_scr[slot, n2:2 * n2, g * gd:(g + 1) * gd] = y[:, gd:].astype(BF16)

    def sequence_stage(k1):
        f = _dot(wseq_ref[...], y_scr[k1 % 2]).astype(BF16)
        o_ref[:, k1 * d_out:(k1 + 1) * d_out] = _dot(f, wf_ref[...]).astype(o_ref.dtype)

    radix_stage(0)
    for k1 in range(FFT_RADIX):
        channel_stage(k1)
        if k1 + 1 < FFT_RADIX:
            radix_stage(k1 + 1)
        sequence_stage(k1)


def _fourier_tables(n):
    n2 = n // FFT_RADIX
    gd = FOURIER_GROUP_DIM
    pos = np.arange(n2, dtype=np.int64)
    k1 = np.arange(1, FFT_RADIX, dtype=np.int64)
    ang = 2.0 * np.pi * ((k1[:, None] * pos[None, :]) % n) / n
    tw_cos = np.broadcast_to(np.cos(ang)[:, :, None], (FFT_RADIX - 1, n2, LANES)).astype(np.float32)
    tw_sin = np.broadcast_to(np.sin(ang)[:, :, None], (FFT_RADIX - 1, n2, LANES)).astype(np.float32)
    ch = np.arange(gd, dtype=np.int64)
    ang_c = 2.0 * np.pi * ((ch[:, None] * ch[None, :]) % gd) / gd
    cc, sc = np.cos(ang_c), np.sin(ang_c)
    w_ch = np.block([[cc, -sc], [sc, cc]]).astype(np.float32)
    kk = np.arange(n2, dtype=np.int64)
    ang_s = 2.0 * np.pi * ((kk[:, None] * kk[None, :]) % n2) / n2
    scale = 1.0 / np.sqrt(float(n) * gd)
    w_seq = (np.concatenate([np.cos(ang_s), np.sin(ang_s)], axis=1) * scale).astype(np.float32)
    return jnp.asarray(tw_cos), jnp.asarray(tw_sin), jnp.asarray(w_ch), jnp.asarray(w_seq)


def _fourier(u, w_f):
    b, n, width = u.shape
    d_out = w_f.shape[1]
    assert n % (FFT_RADIX * 256) == 0 and width == N_FOURIER_GROUPS * FOURIER_GROUP_DIM
    n2 = n // FFT_RADIX
    tw_cos, tw_sin, w_ch, w_seq = _fourier_tables(n)
    kern = functools.partial(_fourier_kernel, n2=n2)
    return pl.pallas_call(
        kern,
        grid=(b,),
        in_specs=[
            pl.BlockSpec((None, n, width), lambda bi: (bi, 0, 0)),
            _resident(tw_cos.shape),
            _resident(tw_sin.shape),
            _resident(w_ch.shape),
            _resident(w_seq.shape),
            _resident(w_f.shape),
        ],
        out_specs=pl.BlockSpec((None, n2, FFT_RADIX * d_out), lambda bi: (bi, 0, 0)),
        out_shape=jax.ShapeDtypeStruct((b, n2, FFT_RADIX * d_out), BF16),
        scratch_shapes=[pltpu.VMEM((2, n2, 2 * width), BF16), pltpu.VMEM((2, 2 * n2, width), BF16)],
        compiler_params=_params(1),
        name="fourier",
    )(u, tw_cos, tw_sin, w_ch.astype(BF16), w_seq.astype(BF16), w_f)


def _rope_tables(n):
    rows = n // GRID_W
    half = HEAD_DIM // 2
    row = jnp.repeat(jnp.arange(rows, dtype=F32), GRID_W)
    col = jnp.tile(jnp.arange(GRID_W, dtype=F32), rows)
    inv_freq = ROPE_BASE ** (-jnp.arange(0, half, 2, dtype=F32) / half)
    ang = jnp.concatenate([row[:, None] * inv_freq, col[:, None] * inv_freq], axis=-1)
    cos, sin = jnp.cos(ang), jnp.sin(ang)
    reps = LANES // HEAD_DIM
    cos_t = jnp.tile(jnp.repeat(cos, 2, axis=-1), (1, reps))
    sin_t = jnp.tile(jnp.stack([-sin, sin], axis=-1).reshape(n, HEAD_DIM), (1, reps))
    return cos_t, sin_t


def kernel(x, c, ctx, c_ctx, w_ada, b_ada, g_ffn1, w_ffn1_in, w_ffn1_out, g_mix, w_in, attn_sink,
           w_attn_branch, w_fourier_branch, w_out, g_ffn2, w_ffn2_in, w_ffn2_out, g_final):
    b, n, d = x.shape
    depth = w_ada.shape[0]
    assert depth == 1, "single-layer problem: the last-layer path of the reference"
    layer = 0
    fourier_width = N_FOURIER_GROUPS * FOURIER_GROUP_DIM

    cond_rows = 8 * ((b + 1 + 7) // 8)
    cond_in = jnp.zeros((cond_rows, d), F32).at[:b].set(c).at[b].set(c_ctx)
    mod = _modulation(cond_in, w_ada[layer], b_ada[layer]).reshape(cond_rows, N_MOD, d)
    mod_lat, mod_ctx = mod[:b], mod[b:b + 1]

    bf = lambda w: w.astype(BF16)
    w_proj = bf(w_in[layer])
    v_off = (N_HEADS + N_KV_HEADS) * HEAD_DIM
    w_vt = w_proj[:, v_off:v_off + N_KV_HEADS * HEAD_DIM].T
    cos_t, sin_t = _rope_tables(n)

    w1_in, w1_out = bf(w_ffn1_in[layer]), bf(w_ffn1_out[layer])
    x1 = _ffn(x, mod_lat, g_ffn1[layer], w1_in, w1_out, base=0, rows=TOKEN_ROWS)
    ctx_flat = ctx.reshape(1, ctx.shape[0] * ctx.shape[1], d)
    ctx_rows = TOKEN_ROWS if ctx_flat.shape[1] % TOKEN_ROWS == 0 else ctx.shape[1]
    ctx1 = _ffn(ctx_flat, mod_ctx, g_ffn1[layer], w1_in, w1_out, base=0, rows=ctx_rows).reshape(ctx.shape)

    q, k2, vt, u, ga, gf = _in_projection(x1, mod_lat, g_mix[layer], w_proj, w_vt, cos_t, sin_t, fourier_width,
                                          rows=TOKEN_ROWS)
    kc2, vct = _ctx_kv(ctx1, mod_ctx, g_mix[layer], w_proj, w_vt)

    ya = _attention(attn_sink[layer], q, k2, vt, kc2, vct, bf(w_attn_branch[layer]), ga)
    fw = _fourier(u, bf(w_fourier_branch[layer]))
    return _mix_ffn(x1, mod_lat, ya, fw, gf, bf(w_out[layer]), g_ffn2[layer], bf(w_ffn2_in[layer]),
                    bf(w_ffn2_out[layer]), g_final, base=6, rows=MIX_ROWS)
```

```python
import functools

import numpy as np
import jax
import jax.numpy as jnp
from jax import lax
from jax.experimental import pallas as pl
from jax.experimental.pallas import tpu as pltpu

N_HEADS = 16
N_KV_HEADS = 4
HEAD_DIM = 64
Q_GROUP = N_HEADS // N_KV_HEADS
WINDOW = 128
GRID_W = 64
N_FOURIER_GROUPS = 4
FOURIER_GROUP_DIM = 128
ROPE_BASE = 10000.0
N_MOD = 9
RMS_EPS = 1e-6
NEG_INF = -1e30

LANES = 128
MXU_DIM = 256
BF16_SUBLANES = 16
VMEM_LIMIT_BYTES = 56 * 1024 * 1024

TOKEN_ROWS = 1024
MIX_ROWS = 512
FF_CHUNK = MXU_DIM
ATTN_ROWS = 2048
Q_BLOCK = 128
ATTN_UNROLL = 8
SCORE_LOOKAHEAD = 2
KEY_BLOCKS = 3
KEY_SPAN = KEY_BLOCKS * Q_BLOCK
VT_ROWS = HEAD_DIM + BF16_SUBLANES
FFT_RADIX = 8

BF16 = jnp.bfloat16
F32 = jnp.float32


def _params(grid_rank):
    return pltpu.CompilerParams(dimension_semantics=("arbitrary",) * grid_rank, vmem_limit_bytes=VMEM_LIMIT_BYTES)


def _dot(a, b):
    return jnp.dot(a, b, preferred_element_type=F32)


def _dot_nt(a, b):
    return lax.dot_general(a, b, (((1,), (1,)), ((), ())), preferred_element_type=F32)


def _rms(x, g):
    return x * lax.rsqrt(jnp.mean(x * x, axis=-1, keepdims=True) + RMS_EPS) * g


def _adaln(x, g, mod_ref, base):
    shift = mod_ref[base:base + 1, :]
    scale = mod_ref[base + 1:base + 2, :]
    return _rms(x, g) * (1.0 + scale) + shift


def _resident(shape):
    zeros = (0,) * len(shape)
    return pl.BlockSpec(shape, lambda *_: zeros)


def _lane_iota(rows):
    return lax.broadcasted_iota(jnp.int32, (rows, LANES), 1)


def _mod_kernel(c_ref, w_ref, b_ref, o_ref):
    c = c_ref[...]
    cond = (c * jax.nn.sigmoid(c)).astype(BF16)
    o_ref[...] = _dot(cond, w_ref[...].astype(BF16)) + b_ref[...]


def _modulation(cond_in, w_ada, b_ada):
    rows, d = cond_in.shape
    n_out = w_ada.shape[1]
    return pl.pallas_call(
        _mod_kernel,
        grid=(n_out // d,),
        in_specs=[
            _resident((rows, d)),
            pl.BlockSpec((d, d), lambda j: (0, j)),
            pl.BlockSpec((1, d), lambda j: (0, j)),
        ],
        out_specs=pl.BlockSpec((rows, d), lambda j: (0, j)),
        out_shape=jax.ShapeDtypeStruct((rows, n_out), F32),
        compiler_params=_params(1),
        name="modulation",
    )(cond_in, w_ada, b_ada.reshape(1, n_out))


def _macaron_ffn(x, mod_ref, g_ref, win_ref, wout_ref, base):
    d_ff = wout_ref.shape[0]
    h = _adaln(x, g_ref[...], mod_ref, base).astype(BF16)
    acc = jnp.zeros(x.shape, F32)
    for c0 in range(0, d_ff, FF_CHUNK):
        a = _dot(h, win_ref[:, c0:c0 + FF_CHUNK])
        u = _dot(h, win_ref[:, d_ff + c0:d_ff + c0 + FF_CHUNK])
        act = (a * jax.nn.sigmoid(a) * u).astype(BF16)
        acc = acc + _dot(act, wout_ref[c0:c0 + FF_CHUNK, :])
    return x + 0.5 * mod_ref[base + 2:base + 3, :] * acc


def _ffn_kernel(x_ref, mod_ref, g_ref, win_ref, wout_ref, o_ref, *, base):
    o_ref[...] = _macaron_ffn(x_ref[...], mod_ref, g_ref, win_ref, wout_ref, base)


def _mix_ffn_kernel(x_ref, mod_ref, ya_ref, fw_ref, gf_ref, wo_ref, g_ref, win_ref, wout_ref, gfin_ref, o_ref,
                    fw_scr, *, base):
    quarter, d = fw_ref.shape[0], x_ref.shape[1]
    groups = fw_scr.shape[0]
    for k1 in range(FFT_RADIX):
        for grp in range(groups):
            cols = slice(k1 * d + grp * LANES, k1 * d + (grp + 1) * LANES)
            fw_scr[grp, pl.ds(k1, quarter, stride=FFT_RADIX), :] = fw_ref[:, cols].astype(F32)
    fw = jnp.concatenate([fw_scr[grp] for grp in range(groups)], axis=-1)
    y = ya_ref[...].astype(F32) + gf_ref[...].astype(F32) * fw
    x = x_ref[...] + mod_ref[5:6, :] * _dot(y.astype(BF16), wo_ref[...])
    out = _macaron_ffn(x, mod_ref, g_ref, win_ref, wout_ref, base)
    o_ref[...] = _rms(out, gfin_ref[...])


def _ffn(x, mod, g, w_in, w_out, *, base, rows):
    b, n, d = x.shape
    d_ff = w_out.shape[0]
    assert n % rows == 0 and d_ff % FF_CHUNK == 0
    per_batch_mod = mod.shape[0] == b
    mod_index = (lambda bi, si: (bi, 0, 0)) if per_batch_mod else (lambda bi, si: (0, 0, 0))
    return pl.pallas_call(
        functools.partial(_ffn_kernel, base=base),
        grid=(b, n // rows),
        in_specs=[
            pl.BlockSpec((None, rows, d), lambda bi, si: (bi, si, 0)),
            pl.BlockSpec((None, N_MOD, d), mod_index),
            _resident((1, d)),
            _resident((d, 2 * d_ff)),
            _resident((d_ff, d)),
        ],
        out_specs=pl.BlockSpec((None, rows, d), lambda bi, si: (bi, si, 0)),
        out_shape=jax.ShapeDtypeStruct((b, n, d), F32),
        compiler_params=_params(2),
        name="ffn",
    )(x, mod, g.reshape(1, d), w_in, w_out)


def _mix_ffn(x, mod, ya, fw, gf, w_o, g, w_in, w_out, g_final, *, base, rows):
    b, n, d = x.shape
    d_ff = w_out.shape[0]
    assert n % rows == 0 and rows % FFT_RADIX == 0 and d_ff % FF_CHUNK == 0 and fw.shape[2] == FFT_RADIX * d
    tok = lambda width: pl.BlockSpec((None, rows, width), lambda bi, si: (bi, si, 0))
    return pl.pallas_call(
        functools.partial(_mix_ffn_kernel, base=base),
        grid=(b, n // rows),
        in_specs=[
            tok(d),
            pl.BlockSpec((None, N_MOD, d), lambda bi, si: (bi, 0, 0)),
            tok(d),
            pl.BlockSpec((None, rows // FFT_RADIX, FFT_RADIX * d), lambda bi, si: (bi, si, 0)),
            tok(d),
            _resident(w_o.shape),
            _resident((1, d)),
            _resident((d, 2 * d_ff)),
            _resident((d_ff, d)),
            _resident((1, d)),
        ],
        out_specs=tok(d),
        out_shape=jax.ShapeDtypeStruct((b, n, d), F32),
        scratch_shapes=[pltpu.VMEM((d // LANES, rows, LANES), F32)],
        compiler_params=_params(2),
        name="mix_ffn_final",
    )(x, mod, ya, fw, gf, w_o, g.reshape(1, d), w_in, w_out, g_final.reshape(1, d))


def _rope(p, cos, sin, even_lane):
    partner = jnp.where(even_lane, pltpu.roll(p, LANES - 1, 1), pltpu.roll(p, 1, 1))
    return p * cos + partner * sin


def _duplicate_heads(pair, low):
    swapped = pltpu.roll(pair, HEAD_DIM, 1)
    return jnp.where(low, pair, swapped), jnp.where(low, swapped, pair)


def _lane_groups(h, w_ref, col, width):
    for c0 in range(0, width, MXU_DIM):
        p = _dot(h, w_ref[:, col + c0:col + c0 + MXU_DIM])
        for half in range(MXU_DIM // LANES):
            yield c0 // LANES + half, p[:, half * LANES:(half + 1) * LANES]


def _store_k(k_ref, h, w_ref, col, width, low, rope):
    for grp, p in _lane_groups(h, w_ref, col, width):
        first, second = _duplicate_heads(rope(p), low)
        k_ref[:, 2 * grp * LANES:(2 * grp + 1) * LANES] = first.astype(k_ref.dtype)
        k_ref[:, (2 * grp + 1) * LANES:(2 * grp + 2) * LANES] = second.astype(k_ref.dtype)


def _store_vt(vt_ref, h, wvt_ref):
    vt = _dot_nt(wvt_ref[...], h)
    ones = jnp.ones((VT_ROWS - HEAD_DIM, Q_BLOCK), vt_ref.dtype)
    for t in range(vt_ref.shape[0]):
        for j in range(N_KV_HEADS):
            vt_ref[t, j * VT_ROWS:j * VT_ROWS + HEAD_DIM, :] = (
                vt[j * HEAD_DIM:(j + 1) * HEAD_DIM, t * Q_BLOCK:(t + 1) * Q_BLOCK].astype(vt_ref.dtype))
            vt_ref[t, j * VT_ROWS + HEAD_DIM:(j + 1) * VT_ROWS, :] = ones


def _inproj_kernel(x_ref, mod_ref, g_ref, w_ref, wvt_ref, cos_ref, sin_ref,
                   q_ref, k_ref, vt_ref, u_ref, ga_ref, gf_ref, *, offsets):
    h = _adaln(x_ref[...], g_ref[...], mod_ref, 3).astype(BF16)
    cos = cos_ref[...]
    sin = sin_ref[...]
    lane = _lane_iota(h.shape[0])
    even_lane = (lane % 2) == 0
    low = lane < HEAD_DIM
    q_scale = HEAD_DIM ** -0.5
    rope = functools.partial(_rope, cos=cos, sin=sin, even_lane=even_lane)
    q_off, k_off, u_off, ga_off, gf_off = offsets

    for grp, p in _lane_groups(h, w_ref, q_off, q_ref.shape[1]):
        q_ref[:, grp * LANES:(grp + 1) * LANES] = (rope(p) * q_scale).astype(q_ref.dtype)
    _store_k(k_ref, h, w_ref, k_off, k_ref.shape[1] // 2, low, rope)
    _store_vt(vt_ref, h, wvt_ref)
    for grp, p in _lane_groups(h, w_ref, u_off, u_ref.shape[1]):
        u_ref[:, grp * LANES:(grp + 1) * LANES] = p.astype(u_ref.dtype)
    for ref, off in ((ga_ref, ga_off), (gf_ref, gf_off)):
        for grp, p in _lane_groups(h, w_ref, off, ref.shape[1]):
            ref[:, grp * LANES:(grp + 1) * LANES] = jax.nn.sigmoid(p).astype(ref.dtype)


def _in_projection(x, mod, g, w, w_vt, cos_t, sin_t, fourier_width, *, rows):
    b, n, d = x.shape
    q_w, kv_w = N_HEADS * HEAD_DIM, N_KV_HEADS * HEAD_DIM
    assert n % rows == 0 and rows % Q_BLOCK == 0 and kv_w % MXU_DIM == 0 and fourier_width % MXU_DIM == 0
    u_off = q_w + 2 * kv_w
    offsets = (0, q_w, u_off, u_off + fourier_width, u_off + fourier_width + d)
    kern = functools.partial(_inproj_kernel, offsets=offsets)
    tok = lambda width: pl.BlockSpec((None, rows, width), lambda si, bi: (bi, si, 0))
    key_blocks = rows // Q_BLOCK
    vt_rows = N_KV_HEADS * VT_ROWS
    out_widths = (q_w, 2 * kv_w, None, fourier_width, d, d)
    out_specs = [tok(width) if width else
                 pl.BlockSpec((None, key_blocks, vt_rows, Q_BLOCK), lambda si, bi: (bi, si, 0, 0))
                 for width in out_widths]
    out_shape = [jax.ShapeDtypeStruct((b, n, width), BF16) if width else
                 jax.ShapeDtypeStruct((b, n // Q_BLOCK, vt_rows, Q_BLOCK), BF16)
                 for width in out_widths]
    return pl.pallas_call(
        kern,
        grid=(n // rows, b),
        in_specs=[
            tok(d),
            pl.BlockSpec((None, N_MOD, d), lambda si, bi: (bi, 0, 0)),
            _resident((1, d)),
            _resident(w.shape),
            _resident(w_vt.shape),
            pl.BlockSpec((rows, LANES), lambda si, bi: (si, 0)),
            pl.BlockSpec((rows, LANES), lambda si, bi: (si, 0)),
        ],
        out_specs=out_specs,
        out_shape=out_shape,
        compiler_params=_params(2),
        name="in_projection",
    )(x, mod, g.reshape(1, d), w, w_vt, cos_t, sin_t)


def _transpose_kernel(w_ref, o_ref):
    o_ref[...] = w_ref[...].T.astype(o_ref.dtype)


def _transposed_columns(w, col, width):
    rows = w.shape[0]
    assert col % width == 0
    return pl.pallas_call(
        _transpose_kernel,
        grid=(1,),
        in_specs=[pl.BlockSpec((rows, width), lambda i: (0, col // width))],
        out_specs=pl.BlockSpec((width, rows), lambda i: (0, 0)),
        out_shape=jax.ShapeDtypeStruct((width, rows), BF16),
        compiler_params=_params(1),
        name="transpose_wv",
    )(w)


def _ctx_kv_kernel(x_ref, mod_ref, g_ref, w_ref, wvt_ref, k_ref, vt_ref, *, k_off):
    h = _adaln(x_ref[...], g_ref[...], mod_ref, 3).astype(BF16)
    low = _lane_iota(h.shape[0]) < HEAD_DIM
    _store_k(k_ref, h, w_ref, k_off, k_ref.shape[1] // 2, low, lambda p: p)
    _store_vt(vt_ref, h, wvt_ref)


def _ctx_kv(ctx, mod_ctx, g, w, w_vt):
    b, n, d = ctx.shape
    kv_w = N_KV_HEADS * HEAD_DIM
    vt_rows = N_KV_HEADS * VT_ROWS
    assert n % Q_BLOCK == 0
    kern = functools.partial(_ctx_kv_kernel, k_off=N_HEADS * HEAD_DIM)
    return pl.pallas_call(
        kern,
        grid=(b,),
        in_specs=[pl.BlockSpec((None, n, d), lambda bi: (bi, 0, 0)),
                  pl.BlockSpec((None, N_MOD, d), lambda bi: (0, 0, 0)),
                  _resident((1, d)), _resident(w.shape), _resident(w_vt.shape)],
        out_specs=[pl.BlockSpec((None, n, 2 * kv_w), lambda bi: (bi, 0, 0)),
                   pl.BlockSpec((None, n // Q_BLOCK, vt_rows, Q_BLOCK), lambda bi: (bi, 0, 0, 0))],
        out_shape=[jax.ShapeDtypeStruct((b, n, 2 * kv_w), BF16),
                   jax.ShapeDtypeStruct((b, n // Q_BLOCK, vt_rows, Q_BLOCK), BF16)],
        compiler_params=_params(1),
        name="ctx_kv",
    )(ctx, mod_ctx, g.reshape(1, d), w, w_vt)


def _attn_kernel(sink_ref, q_ref, k_ref, vt_ref, kc_ref, vct_ref, wa_ref, ga_ref, o_ref, *, seq):
    tile_start = pl.program_id(1) * ATTN_ROWS
    lane = _lane_iota(Q_BLOCK)
    keep_low = (lane < HEAD_DIM).astype(BF16)
    keep_high = 1.0 - keep_low
    key_row = lax.broadcasted_iota(jnp.int32, (KEY_SPAN, Q_BLOCK), 0)
    query_col = lax.broadcasted_iota(jnp.int32, (KEY_SPAN, Q_BLOCK), 1)
    head_of_lane = lax.broadcasted_iota(jnp.int32, (1, Q_GROUP * Q_BLOCK), 1) // Q_BLOCK
    n_ctx_blocks = vct_ref.shape[0]

    def window(sb):
        q_start = tile_start + sb * Q_BLOCK
        k_start = pl.multiple_of(jnp.clip(q_start - Q_BLOCK, 0, seq - KEY_SPAN), Q_BLOCK)
        dist = key_row - query_col + (k_start - q_start)
        bias = jnp.where(jnp.abs(dist) <= WINDOW, 0.0, NEG_INF).astype(F32)
        return pl.multiple_of(sb * Q_BLOCK, Q_BLOCK), k_start, bias

    def scores(win, j):
        r0, k_start, bias = win
        ks = slice(j * LANES, (j + 1) * LANES)
        qa = q_ref[pl.ds(r0, Q_BLOCK), 2 * j * LANES:(2 * j + 1) * LANES]
        qb = q_ref[pl.ds(r0, Q_BLOCK), (2 * j + 1) * LANES:(2 * j + 2) * LANES]
        q4 = jnp.concatenate([qa * keep_low, qa * keep_high, qb * keep_low, qb * keep_high], axis=0)
        keys = jnp.concatenate([k_ref[pl.ds(k_start, KEY_SPAN), ks], kc_ref[:, ks]], axis=0)
        s_all = _dot_nt(keys, q4)
        s_loc = jnp.concatenate([s_all[:KEY_SPAN, g * Q_BLOCK:(g + 1) * Q_BLOCK] + bias for g in range(Q_GROUP)],
                                axis=1)
        return s_loc.astype(BF16), s_all[KEY_SPAN:].astype(BF16)

    def block(step, carry):
        wins = [window(step * ATTN_UNROLL + i) for i in range(ATTN_UNROLL)]
        units = [(wi, j) for wi in range(ATTN_UNROLL) for j in range(N_KV_HEADS)]
        head_groups = {}
        piece = wa_ref.shape[1] // N_KV_HEADS

        def project(wi, c):
            attn_rows = jnp.concatenate(head_groups[wi], axis=1)
            rows, cols = pl.ds(wins[wi][0], Q_BLOCK), slice(c * piece, (c + 1) * piece)
            gated = ga_ref[rows, cols].astype(F32) * _dot(attn_rows, wa_ref[:, cols])
            o_ref[rows, cols] = gated.astype(o_ref.dtype)

        pending = [scores(wins[wi], j) for wi, j in units[:SCORE_LOOKAHEAD]]
        for idx, (wi, j) in enumerate(units):
            k_block = wins[wi][1] // Q_BLOCK
            s_loc, s_ctx = pending.pop(0)
            if idx + SCORE_LOOKAHEAD < len(units):
                nwi, nj = units[idx + SCORE_LOOKAHEAD]
                pending.append(scores(wins[nwi], nj))
            if wi > 0:
                project(wi - 1, j)
            vs = slice(j * VT_ROWS, (j + 1) * VT_ROWS)
            sink = jnp.zeros((1, Q_GROUP * Q_BLOCK), F32)
            for gi in range(Q_GROUP):
                sink = jnp.where(head_of_lane == gi, sink_ref[j * Q_GROUP + gi], sink)
            top = jnp.maximum(jnp.max(s_loc, axis=0, keepdims=True), jnp.max(s_ctx, axis=0, keepdims=True))
            m_b = jnp.maximum(top.astype(F32), sink).astype(BF16)
            m = m_b.astype(F32)
            p_all = jnp.concatenate([jnp.exp(s_loc - m_b), jnp.exp(s_ctx - m_b)], axis=0)
            vt_all = jnp.concatenate([vt_ref[k_block + t, vs, :] for t in range(KEY_BLOCKS)]
                                     + [vct_ref[t, vs, :] for t in range(n_ctx_blocks)], axis=1)
            out_t = _dot(vt_all, p_all)
            denom = out_t[HEAD_DIM:HEAD_DIM + 1, :] + jnp.exp(sink - m)
            vals = out_t[:HEAD_DIM, :] / denom
            for pair in range(Q_GROUP // 2):
                two_heads = jnp.concatenate([vals[:, (2 * pair) * Q_BLOCK:(2 * pair + 1) * Q_BLOCK],
                                             vals[:, (2 * pair + 1) * Q_BLOCK:(2 * pair + 2) * Q_BLOCK]], axis=0)
                head_groups.setdefault(wi, []).append(two_heads.T.astype(BF16))
        for c in range(N_KV_HEADS):
            project(ATTN_UNROLL - 1, c)
        return carry

    lax.fori_loop(0, ATTN_ROWS // (Q_BLOCK * ATTN_UNROLL), block, 0)


def _attention(sink, q, k2, vt, kc2, vct, w_a, ga):
    b, n, dq = q.shape
    n_ctx, d_out = kc2.shape[1], w_a.shape[1]
    assert n % ATTN_ROWS == 0 and n >= KEY_SPAN and n_ctx % Q_BLOCK == 0 and ATTN_ROWS % (Q_BLOCK * ATTN_UNROLL) == 0
    kern = functools.partial(_attn_kernel, seq=n)
    per_sample = lambda arr: pl.BlockSpec((None,) + arr.shape[1:], lambda bi, si: (bi,) + (0,) * (arr.ndim - 1))
    return pl.pallas_call(
        kern,
        grid=(b, n // ATTN_ROWS),
        in_specs=[
            pl.BlockSpec(memory_space=pltpu.SMEM),
            pl.BlockSpec((None, ATTN_ROWS, dq), lambda bi, si: (bi, si, 0)),
            per_sample(k2), per_sample(vt), per_sample(kc2), per_sample(vct),
            _resident(w_a.shape),
            pl.BlockSpec((None, ATTN_ROWS, d_out), lambda bi, si: (bi, si, 0)),
        ],
        out_specs=pl.BlockSpec((None, ATTN_ROWS, d_out), lambda bi, si: (bi, si, 0)),
        out_shape=jax.ShapeDtypeStruct((b, n, d_out), BF16),
        compiler_params=_params(2),
        name="attention",
    )(sink, q, k2, vt, kc2, vct, w_a, ga)


def _weighted_sum(terms, coefs, eps=1e-9):
    by_magnitude = {}
    for term, coef in zip(terms, coefs):
        if abs(coef) > eps:
            plus, minus = by_magnitude.setdefault(round(abs(coef), 9), ([], []))
            (plus if coef > 0 else minus).append(term)
    total = None
    for magnitude, (plus, minus) in by_magnitude.items():
        part = functools.reduce(lambda a, b: a + b, plus) if plus else None
        if minus:
            neg = functools.reduce(lambda a, b: a + b, minus)
            part = -neg if part is None else part - neg
        if abs(magnitude - 1.0) > eps:
            part = part * magnitude
        total = part if total is None else total + part
    return total


def _fourier_kernel(u_ref, tc_ref, ts_ref, wch_ref, wseq_ref, wf_ref, o_ref, v_scr, y_scr, *, n2):
    gd = FOURIER_GROUP_DIM
    d_out = wf_ref.shape[1]
    row_chunk = 256

    pairs = [(k1,) if (FFT_RADIX - k1) % FFT_RADIX == k1 else (k1, FFT_RADIX - k1)
             for k1 in range(FFT_RADIX // 2 + 1)]
    slot_of = {k1: (2 * idx + pos) % v_scr.shape[0] for idx, pair in enumerate(pairs) for pos, k1 in enumerate(pair)}

    def radix_stage(pair):
        k1 = pair[0]
        turns = [2.0 * np.pi * ((n1 * k1) % FFT_RADIX) / FFT_RADIX for n1 in range(FFT_RADIX)]
        re_coef = [float(np.cos(t)) for t in turns]
        im_coef = [-float(np.sin(t)) for t in turns]
        for g in range(N_FOURIER_GROUPS):
            for r0 in range(0, n2, row_chunk):
                us = [u_ref[n1 * n2 + r0:n1 * n2 + r0 + row_chunk, g * gd:(g + 1) * gd].astype(F32)
                      for n1 in range(FFT_RADIX)]
                er = _weighted_sum(us, re_coef)
                ei = _weighted_sum(us, im_coef)
                for pos, k in enumerate(pair):
                    ek = ei if pos == 0 or ei is None else -ei
                    if k == 0:
                        vr, vi = er, jnp.zeros_like(er)
                    else:
                        c = tc_ref[k - 1, r0:r0 + row_chunk, :]
                        s = ts_ref[k - 1, r0:r0 + row_chunk, :]
                        vr, vi = (er * c, -(er * s)) if ek is None else (er * c + ek * s, ek * c - er * s)
                    v_scr[slot_of[k], r0:r0 + row_chunk, 2 * g * gd:(2 * g + 1) * gd] = vr.astype(BF16)
                    v_scr[slot_of[k], r0:r0 + row_chunk, (2 * g + 1) * gd:(2 * g + 2) * gd] = vi.astype(BF16)

    def channel_stage(k1):
        slot = slot_of[k1]
        for g in range(N_FOURIER_GROUPS):
            y = _dot(v_scr[slot, :, 2 * g * gd:(2 * g + 2) * gd], wch_ref[...])
            y_scr[slot, 0:n2, g * gd:(g + 1) * gd] = y[:, :gd].astype(BF16)
            y_scr[slot, n2:2 * n2, g * gd:(g + 1) * gd] = y[:, gd:].astype(BF16)

    def sequence_stage(k1):
        f = _dot(wseq_ref[...], y_scr[slot_of[k1]]).astype(BF16)
        o_ref[:, k1 * d_out:(k1 + 1) * d_out] = _dot(f, wf_ref[...]).astype(o_ref.dtype)

    radix_stage(pairs[0])
    for idx, pair in enumerate(pairs):
        for pos, k1 in enumerate(pair):
            channel_stage(k1)
            if pos == 0 and idx + 1 < len(pairs):
                radix_stage(pairs[idx + 1])
            sequence_stage(k1)


def _fourier_tables(n):
    n2 = n // FFT_RADIX
    gd = FOURIER_GROUP_DIM
    pos = np.arange(n2, dtype=np.int64)
    k1 = np.arange(1, FFT_RADIX, dtype=np.int64)
    ang = 2.0 * np.pi * ((k1[:, None] * pos[None, :]) % n) / n
    tw_cos = np.broadcast_to(np.cos(ang)[:, :, None], (FFT_RADIX - 1, n2, LANES)).astype(np.float32)
    tw_sin = np.broadcast_to(np.sin(ang)[:, :, None], (FFT_RADIX - 1, n2, LANES)).astype(np.float32)
    ch = np.arange(gd, dtype=np.int64)
    ang_c = 2.0 * np.pi * ((ch[:, None] * ch[None, :]) % gd) / gd
    cc, sc = np.cos(ang_c), np.sin(ang_c)
    w_ch = np.block([[cc, -sc], [sc, cc]]).astype(np.float32)
    kk = np.arange(n2, dtype=np.int64)
    ang_s = 2.0 * np.pi * ((kk[:, None] * kk[None, :]) % n2) / n2
    scale = 1.0 / np.sqrt(float(n) * gd)
    w_seq = (np.concatenate([np.cos(ang_s), np.sin(ang_s)], axis=1) * scale).astype(np.float32)
    return jnp.asarray(tw_cos), jnp.asarray(tw_sin), jnp.asarray(w_ch), jnp.asarray(w_seq)


def _fourier(u, w_f):
    b, n, width = u.shape
    d_out = w_f.shape[1]
    assert n % (FFT_RADIX * 256) == 0 and width == N_FOURIER_GROUPS * FOURIER_GROUP_DIM
    n2 = n // FFT_RADIX
    tw_cos, tw_sin, w_ch, w_seq = _fourier_tables(n)
    kern = functools.partial(_fourier_kernel, n2=n2)
    return pl.pallas_call(
        kern,
        grid=(b,),
        in_specs=[
            pl.BlockSpec((None, n, width), lambda bi: (bi, 0, 0)),
            _resident(tw_cos.shape),
            _resident(tw_sin.shape),
            _resident(w_ch.shape),
            _resident(w_seq.shape),
            _resident(w_f.shape),
        ],
        out_specs=pl.BlockSpec((None, n2, FFT_RADIX * d_out), lambda bi: (bi, 0, 0)),
        out_shape=jax.ShapeDtypeStruct((b, n2, FFT_RADIX * d_out), BF16),
        scratch_shapes=[pltpu.VMEM((4, n2, 2 * width), BF16), pltpu.VMEM((4, 2 * n2, width), BF16)],
        compiler_params=_params(1),
        name="fourier",
    )(u, tw_cos, tw_sin, w_ch.astype(BF16), w_seq.astype(BF16), w_f)


def _rope_tables(n):
    rows = n // GRID_W
    half = HEAD_DIM // 2
    row = jnp.repeat(jnp.arange(rows, dtype=F32), GRID_W)
    col = jnp.tile(jnp.arange(GRID_W, dtype=F32), rows)
    inv_freq = ROPE_BASE ** (-jnp.arange(0, half, 2, dtype=F32) / half)
    ang = jnp.concatenate([row[:, None] * inv_freq, col[:, None] * inv_freq], axis=-1)
    cos, sin = jnp.cos(ang), jnp.sin(ang)
    reps = LANES // HEAD_DIM
    cos_t = jnp.tile(jnp.repeat(cos, 2, axis=-1), (1, reps))
    sin_t = jnp.tile(jnp.stack([-sin, sin], axis=-1).reshape(n, HEAD_DIM), (1, reps))
    return cos_t, sin_t


def kernel(x, c, ctx, c_ctx, w_ada, b_ada, g_ffn1, w_ffn1_in, w_ffn1_out, g_mix, w_in, attn_sink,
           w_attn_branch, w_fourier_branch, w_out, g_ffn2, w_ffn2_in, w_ffn2_out, g_final):
    b, n, d = x.shape
    depth = w_ada.shape[0]
    assert depth == 1, "single-layer problem: the last-layer path of the reference"
    layer = 0
    fourier_width = N_FOURIER_GROUPS * FOURIER_GROUP_DIM

    cond_rows = 8 * ((b + 1 + 7) // 8)
    cond_in = jnp.zeros((cond_rows, d), F32).at[:b].set(c).at[b].set(c_ctx)
    mod = _modulation(cond_in, w_ada[layer], b_ada[layer]).reshape(cond_rows, N_MOD, d)
    mod_lat, mod_ctx = mod[:b], mod[b:b + 1]

    bf = lambda w: w.astype(BF16)
    w_proj = bf(w_in[layer])
    v_off = (N_HEADS + N_KV_HEADS) * HEAD_DIM
    w_vt = _transposed_columns(w_in[layer], v_off, N_KV_HEADS * HEAD_DIM)
    cos_t, sin_t = _rope_tables(n)

    w1_in, w1_out = bf(w_ffn1_in[layer]), bf(w_ffn1_out[layer])
    x1 = _ffn(x, mod_lat, g_ffn1[layer], w1_in, w1_out, base=0, rows=TOKEN_ROWS)
    ctx_flat = ctx.reshape(1, ctx.shape[0] * ctx.shape[1], d)
    ctx_rows = TOKEN_ROWS if ctx_flat.shape[1] % TOKEN_ROWS == 0 else ctx.shape[1]
    ctx1 = _ffn(ctx_flat, mod_ctx, g_ffn1[layer], w1_in, w1_out, base=0, rows=ctx_rows).reshape(ctx.shape)

    q, k2, vt, u, ga, gf = _in_projection(x1, mod_lat, g_mix[layer], w_proj, w_vt, cos_t, sin_t, fourier_width,
                                          rows=TOKEN_ROWS)
    kc2, vct = _ctx_kv(ctx1, mod_ctx, g_mix[layer], w_proj, w_vt)

    ya = _attention(attn_sink[layer], q, k2, vt, kc2, vct, bf(w_attn_branch[layer]), ga)
    fw = _fourier(u, bf(w_fourier_branch[layer]))
    return _mix_ffn(x1, mod_lat, ya, fw, gf, bf(w_out[layer]), g_ffn2[layer], bf(w_ffn2_in[layer]),
                    bf(w_ffn2_out[layer]), g_final, base=6, rows=MIX_ROWS)
```

```python
import functools

import numpy as np
import jax
import jax.numpy as jnp
from jax import lax
from jax.experimental import pallas as pl
from jax.experimental.pallas import tpu as pltpu

N_HEADS = 16
N_KV_HEADS = 4
HEAD_DIM = 64
Q_GROUP = N_HEADS // N_KV_HEADS
WINDOW = 128
GRID_W = 64
N_FOURIER_GROUPS = 4
FOURIER_GROUP_DIM = 128
ROPE_BASE = 10000.0
N_MOD = 9
RMS_EPS = 1e-6
NEG_INF = -1e30

LANES = 128
MXU_DIM = 256
BF16_SUBLANES = 16
VMEM_LIMIT_BYTES = 56 * 1024 * 1024

TOKEN_ROWS = 1024
MIX_ROWS = 512
FF_CHUNK = MXU_DIM
ATTN_ROWS = 1024
Q_BLOCK = 128
ATTN_UNROLL = 8
SCORE_LOOKAHEAD = 2
KEY_BLOCKS = 3
KEY_SPAN = KEY_BLOCKS * Q_BLOCK
VT_ROWS = HEAD_DIM + BF16_SUBLANES
FFT_RADIX = 8

BF16 = jnp.bfloat16
F32 = jnp.float32


def _params(grid_rank):
    return pltpu.CompilerParams(dimension_semantics=("arbitrary",) * grid_rank, vmem_limit_bytes=VMEM_LIMIT_BYTES)


def _dot(a, b):
    return jnp.dot(a, b, preferred_element_type=F32)


def _dot_nt(a, b):
    return lax.dot_general(a, b, (((1,), (1,)), ((), ())), preferred_element_type=F32)


def _rms(x, g):
    return x * lax.rsqrt(jnp.mean(x * x, axis=-1, keepdims=True) + RMS_EPS) * g


def _adaln(x, g, mod_ref, base):
    shift = mod_ref[base:base + 1, :]
    scale = mod_ref[base + 1:base + 2, :]
    return _rms(x, g) * (1.0 + scale) + shift


def _resident(shape):
    zeros = (0,) * len(shape)
    return pl.BlockSpec(shape, lambda *_: zeros)


def _lane_iota(rows):
    return lax.broadcasted_iota(jnp.int32, (rows, LANES), 1)


def _mod_kernel(c_ref, w_ref, b_ref, o_ref):
    c = c_ref[...]
    cond = (c * jax.nn.sigmoid(c)).astype(BF16)
    o_ref[...] = _dot(cond, w_ref[...].astype(BF16)) + b_ref[...]


def _modulation(cond_in, w_ada, b_ada):
    rows, d = cond_in.shape
    n_out = w_ada.shape[1]
    return pl.pallas_call(
        _mod_kernel,
        grid=(n_out // d,),
        in_specs=[
            _resident((rows, d)),
            pl.BlockSpec((d, d), lambda j: (0, j)),
            pl.BlockSpec((1, d), lambda j: (0, j)),
        ],
        out_specs=pl.BlockSpec((rows, d), lambda j: (0, j)),
        out_shape=jax.ShapeDtypeStruct((rows, n_out), F32),
        compiler_params=_params(1),
        name="modulation",
    )(cond_in, w_ada, b_ada.reshape(1, n_out))


def _macaron_ffn(x, mod_ref, g_ref, win_ref, wout_ref, base):
    d_ff = wout_ref.shape[0]
    h = _adaln(x, g_ref[...], mod_ref, base).astype(BF16)
    acc = jnp.zeros(x.shape, F32)
    for c0 in range(0, d_ff, FF_CHUNK):
        a = _dot(h, win_ref[:, c0:c0 + FF_CHUNK])
        u = _dot(h, win_ref[:, d_ff + c0:d_ff + c0 + FF_CHUNK])
        act = (a * jax.nn.sigmoid(a) * u).astype(BF16)
        acc = acc + _dot(act, wout_ref[c0:c0 + FF_CHUNK, :])
    return x + 0.5 * mod_ref[base + 2:base + 3, :] * acc


def _ffn_kernel(x_ref, mod_ref, g_ref, win_ref, wout_ref, o_ref, *, base):
    o_ref[...] = _macaron_ffn(x_ref[...], mod_ref, g_ref, win_ref, wout_ref, base)


def _mix_ffn_kernel(x_ref, mod_ref, ya_ref, fw_ref, gf_ref, wo_ref, g_ref, win_ref, wout_ref, gfin_ref, o_ref,
                    fw_scr, *, base):
    quarter, d = fw_ref.shape[0], x_ref.shape[1]
    groups = fw_scr.shape[0]
    for k1 in range(FFT_RADIX):
        for grp in range(groups):
            cols = slice(k1 * d + grp * LANES, k1 * d + (grp + 1) * LANES)
            fw_scr[grp, pl.ds(k1, quarter, stride=FFT_RADIX), :] = fw_ref[:, cols].astype(F32)
    fw = jnp.concatenate([fw_scr[grp] for grp in range(groups)], axis=-1)
    y = ya_ref[...].astype(F32) + gf_ref[...].astype(F32) * fw
    x = x_ref[...] + mod_ref[5:6, :] * _dot(y.astype(BF16), wo_ref[...])
    out = _macaron_ffn(x, mod_ref, g_ref, win_ref, wout_ref, base)
    o_ref[...] = _rms(out, gfin_ref[...])


def _ffn(x, mod, g, w_in, w_out, *, base, rows):
    b, n, d = x.shape
    d_ff = w_out.shape[0]
    assert n % rows == 0 and d_ff % FF_CHUNK == 0
    per_batch_mod = mod.shape[0] == b
    mod_index = (lambda bi, si: (bi, 0, 0)) if per_batch_mod else (lambda bi, si: (0, 0, 0))
    return pl.pallas_call(
        functools.partial(_ffn_kernel, base=base),
        grid=(b, n // rows),
        in_specs=[
            pl.BlockSpec((None, rows, d), lambda bi, si: (bi, si, 0)),
            pl.BlockSpec((None, N_MOD, d), mod_index),
            _resident((1, d)),
            _resident((d, 2 * d_ff)),
            _resident((d_ff, d)),
        ],
        out_specs=pl.BlockSpec((None, rows, d), lambda bi, si: (bi, si, 0)),
        out_shape=jax.ShapeDtypeStruct((b, n, d), F32),
        compiler_params=_params(2),
        name="ffn",
    )(x, mod, g.reshape(1, d), w_in, w_out)


def _mix_ffn(x, mod, ya, fw, gf, w_o, g, w_in, w_out, g_final, *, base, rows):
    b, n, d = x.shape
    d_ff = w_out.shape[0]
    assert n % rows == 0 and rows % FFT_RADIX == 0 and d_ff % FF_CHUNK == 0 and fw.shape[2] == FFT_RADIX * d
    tok = lambda width: pl.BlockSpec((None, rows, width), lambda bi, si: (bi, si, 0))
    return pl.pallas_call(
        functools.partial(_mix_ffn_kernel, base=base),
        grid=(b, n // rows),
        in_specs=[
            tok(d),
            pl.BlockSpec((None, N_MOD, d), lambda bi, si: (bi, 0, 0)),
            tok(d),
            pl.BlockSpec((None, rows // FFT_RADIX, FFT_RADIX * d), lambda bi, si: (bi, si, 0)),
            tok(d),
            _resident(w_o.shape),
            _resident((1, d)),
            _resident((d, 2 * d_ff)),
            _resident((d_ff, d)),
            _resident((1, d)),
        ],
        out_specs=tok(d),
        out_shape=jax.ShapeDtypeStruct((b, n, d), F32),
        scratch_shapes=[pltpu.VMEM((d // LANES, rows, LANES), F32)],
        compiler_params=_params(2),
        name="mix_ffn_final",
    )(x, mod, ya, fw, gf, w_o, g.reshape(1, d), w_in, w_out, g_final.reshape(1, d))


def _rope(p, cos, sin, even_lane):
    partner = jnp.where(even_lane, pltpu.roll(p, LANES - 1, 1), pltpu.roll(p, 1, 1))
    return p * cos + partner * sin


def _duplicate_heads(pair, low):
    swapped = pltpu.roll(pair, HEAD_DIM, 1)
    return jnp.where(low, pair, swapped), jnp.where(low, swapped, pair)


def _lane_groups(h, w_ref, col, width):
    for c0 in range(0, width, MXU_DIM):
        p = _dot(h, w_ref[:, col + c0:col + c0 + MXU_DIM])
        for half in range(MXU_DIM // LANES):
            yield c0 // LANES + half, p[:, half * LANES:(half + 1) * LANES]


def _store_k(k_ref, h, w_ref, col, width, low, rope):
    for grp, p in _lane_groups(h, w_ref, col, width):
        first, second = _duplicate_heads(rope(p), low)
        k_ref[:, 2 * grp * LANES:(2 * grp + 1) * LANES] = first.astype(k_ref.dtype)
        k_ref[:, (2 * grp + 1) * LANES:(2 * grp + 2) * LANES] = second.astype(k_ref.dtype)


def _store_vt(vt_ref, h, wvt_ref):
    vt = _dot_nt(wvt_ref[...], h)
    ones = jnp.ones((VT_ROWS - HEAD_DIM, Q_BLOCK), vt_ref.dtype)
    for t in range(vt_ref.shape[0]):
        for j in range(N_KV_HEADS):
            vt_ref[t, j * VT_ROWS:j * VT_ROWS + HEAD_DIM, :] = (
                vt[j * HEAD_DIM:(j + 1) * HEAD_DIM, t * Q_BLOCK:(t + 1) * Q_BLOCK].astype(vt_ref.dtype))
            vt_ref[t, j * VT_ROWS + HEAD_DIM:(j + 1) * VT_ROWS, :] = ones


def _inproj_kernel(x_ref, mod_ref, g_ref, w_ref, wvt_ref, cos_ref, sin_ref,
                   q_ref, k_ref, vt_ref, u_ref, ga_ref, gf_ref, *, offsets):
    h = _adaln(x_ref[...], g_ref[...], mod_ref, 3).astype(BF16)
    cos = cos_ref[...]
    sin = sin_ref[...]
    lane = _lane_iota(h.shape[0])
    even_lane = (lane % 2) == 0
    low = lane < HEAD_DIM
    q_scale = HEAD_DIM ** -0.5
    rope = functools.partial(_rope, cos=cos, sin=sin, even_lane=even_lane)
    q_off, k_off, u_off, ga_off, gf_off = offsets

    for grp, p in _lane_groups(h, w_ref, q_off, q_ref.shape[1]):
        q_ref[:, grp * LANES:(grp + 1) * LANES] = (rope(p) * q_scale).astype(q_ref.dtype)
    _store_k(k_ref, h, w_ref, k_off, k_ref.shape[1] // 2, low, rope)
    _store_vt(vt_ref, h, wvt_ref)
    for grp, p in _lane_groups(h, w_ref, u_off, u_ref.shape[1]):
        u_ref[:, grp * LANES:(grp + 1) * LANES] = p.astype(u_ref.dtype)
    for ref, off in ((ga_ref, ga_off), (gf_ref, gf_off)):
        for grp, p in _lane_groups(h, w_ref, off, ref.shape[1]):
            ref[:, grp * LANES:(grp + 1) * LANES] = jax.nn.sigmoid(p).astype(ref.dtype)


def _in_projection(x, mod, g, w, w_vt, cos_t, sin_t, fourier_width, *, rows):
    b, n, d = x.shape
    q_w, kv_w = N_HEADS * HEAD_DIM, N_KV_HEADS * HEAD_DIM
    assert n % rows == 0 and rows % Q_BLOCK == 0 and kv_w % MXU_DIM == 0 and fourier_width % MXU_DIM == 0
    u_off = q_w + 2 * kv_w
    offsets = (0, q_w, u_off, u_off + fourier_width, u_off + fourier_width + d)
    kern = functools.partial(_inproj_kernel, offsets=offsets)
    tok = lambda width: pl.BlockSpec((None, rows, width), lambda si, bi: (bi, si, 0))
    key_blocks = rows // Q_BLOCK
    vt_rows = N_KV_HEADS * VT_ROWS
    out_widths = (q_w, 2 * kv_w, None, fourier_width, d, d)
    out_specs = [tok(width) if width else
                 pl.BlockSpec((None, key_blocks, vt_rows, Q_BLOCK), lambda si, bi: (bi, si, 0, 0))
                 for width in out_widths]
    out_shape = [jax.ShapeDtypeStruct((b, n, width), BF16) if width else
                 jax.ShapeDtypeStruct((b, n // Q_BLOCK, vt_rows, Q_BLOCK), BF16)
                 for width in out_widths]
    return pl.pallas_call(
        kern,
        grid=(n // rows, b),
        in_specs=[
            tok(d),
            pl.BlockSpec((None, N_MOD, d), lambda si, bi: (bi, 0, 0)),
            _resident((1, d)),
            _resident(w.shape),
            _resident(w_vt.shape),
            pl.BlockSpec((rows, LANES), lambda si, bi: (si, 0)),
            pl.BlockSpec((rows, LANES), lambda si, bi: (si, 0)),
        ],
        out_specs=out_specs,
        out_shape=out_shape,
        compiler_params=_params(2),
        name="in_projection",
    )(x, mod, g.reshape(1, d), w, w_vt, cos_t, sin_t)


def _transpose_kernel(w_ref, o_ref):
    o_ref[...] = w_ref[...].T.astype(o_ref.dtype)


def _transposed_columns(w, col, width):
    rows = w.shape[0]
    assert col % width == 0
    return pl.pallas_call(
        _transpose_kernel,
        grid=(1,),
        in_specs=[pl.BlockSpec((rows, width), lambda i: (0, col // width))],
        out_specs=pl.BlockSpec((width, rows), lambda i: (0, 0)),
        out_shape=jax.ShapeDtypeStruct((width, rows), BF16),
        compiler_params=_params(1),
        name="transpose_wv",
    )(w)


def _ctx_kv_kernel(x_ref, mod_ref, g_ref, w_ref, wvt_ref, k_ref, vt_ref, *, k_off):
    h = _adaln(x_ref[...], g_ref[...], mod_ref, 3).astype(BF16)
    low = _lane_iota(h.shape[0]) < HEAD_DIM
    _store_k(k_ref, h, w_ref, k_off, k_ref.shape[1] // 2, low, lambda p: p)
    _store_vt(vt_ref, h, wvt_ref)


def _ctx_kv(ctx, mod_ctx, g, w, w_vt):
    b, n, d = ctx.shape
    kv_w = N_KV_HEADS * HEAD_DIM
    vt_rows = N_KV_HEADS * VT_ROWS
    assert n % Q_BLOCK == 0
    kern = functools.partial(_ctx_kv_kernel, k_off=N_HEADS * HEAD_DIM)
    return pl.pallas_call(
        kern,
        grid=(b,),
        in_specs=[pl.BlockSpec((None, n, d), lambda bi: (bi, 0, 0)),
                  pl.BlockSpec((None, N_MOD, d), lambda bi: (0, 0, 0)),
                  _resident((1, d)), _resident(w.shape), _resident(w_vt.shape)],
        out_specs=[pl.BlockSpec((None, n, 2 * kv_w), lambda bi: (bi, 0, 0)),
                   pl.BlockSpec((None, n // Q_BLOCK, vt_rows, Q_BLOCK), lambda bi: (bi, 0, 0, 0))],
        out_shape=[jax.ShapeDtypeStruct((b, n, 2 * kv_w), BF16),
                   jax.ShapeDtypeStruct((b, n // Q_BLOCK, vt_rows, Q_BLOCK), BF16)],
        compiler_params=_params(1),
        name="ctx_kv",
    )(ctx, mod_ctx, g.reshape(1, d), w, w_vt)


def _attn_kernel(sink_ref, q_ref, k_ref, vt_ref, kc_ref, vct_ref, wa_ref, ga_ref, o_ref, *, seq):
    tile_start = pl.program_id(1) * ATTN_ROWS
    lane = _lane_iota(Q_BLOCK)
    keep_low = (lane < HEAD_DIM).astype(BF16)
    keep_high = 1.0 - keep_low
    key_row = lax.broadcasted_iota(jnp.int32, (KEY_SPAN, Q_BLOCK), 0)
    query_col = lax.broadcasted_iota(jnp.int32, (KEY_SPAN, Q_BLOCK), 1)
    head_of_lane = lax.broadcasted_iota(jnp.int32, (1, Q_GROUP * Q_BLOCK), 1) // Q_BLOCK
    n_ctx_blocks = vct_ref.shape[0]

    def window(sb):
        q_start = tile_start + sb * Q_BLOCK
        k_start = pl.multiple_of(jnp.clip(q_start - Q_BLOCK, 0, seq - KEY_SPAN), Q_BLOCK)
        dist = key_row - query_col + (k_start - q_start)
        bias = jnp.where(jnp.abs(dist) <= WINDOW, 0.0, NEG_INF).astype(F32)
        return pl.multiple_of(sb * Q_BLOCK, Q_BLOCK), k_start, bias

    def scores(win, j):
        r0, k_start, bias = win
        ks = slice(j * LANES, (j + 1) * LANES)
        qa = q_ref[pl.ds(r0, Q_BLOCK), 2 * j * LANES:(2 * j + 1) * LANES]
        qb = q_ref[pl.ds(r0, Q_BLOCK), (2 * j + 1) * LANES:(2 * j + 2) * LANES]
        q4 = jnp.concatenate([qa * keep_low, qa * keep_high, qb * keep_low, qb * keep_high], axis=0)
        keys = jnp.concatenate([k_ref[pl.ds(k_start, KEY_SPAN), ks], kc_ref[:, ks]], axis=0)
        s_all = _dot_nt(keys, q4)
        s_loc = jnp.concatenate([s_all[:KEY_SPAN, g * Q_BLOCK:(g + 1) * Q_BLOCK] + bias for g in range(Q_GROUP)],
                                axis=1)
        return s_loc.astype(BF16), s_all[KEY_SPAN:].astype(BF16)

    def block(step, carry):
        wins = [window(step * ATTN_UNROLL + i) for i in range(ATTN_UNROLL)]
        units = [(wi, j) for wi in range(ATTN_UNROLL) for j in range(N_KV_HEADS)]
        head_groups = {}
        piece = wa_ref.shape[1] // N_KV_HEADS

        def project(wi, c):
            attn_rows = jnp.concatenate(head_groups[wi], axis=1)
            rows, cols = pl.ds(wins[wi][0], Q_BLOCK), slice(c * piece, (c + 1) * piece)
            gated = ga_ref[rows, cols].astype(F32) * _dot(attn_rows, wa_ref[:, cols])
            o_ref[rows, cols] = gated.astype(o_ref.dtype)

        pending = [scores(wins[wi], j) for wi, j in units[:SCORE_LOOKAHEAD]]
        for idx, (wi, j) in enumerate(units):
            k_block = wins[wi][1] // Q_BLOCK
            s_loc, s_ctx = pending.pop(0)
            if idx + SCORE_LOOKAHEAD < len(units):
                nwi, nj = units[idx + SCORE_LOOKAHEAD]
                pending.append(scores(wins[nwi], nj))
            if wi > 0:
                project(wi - 1, j)
            vs = slice(j * VT_ROWS, (j + 1) * VT_ROWS)
            sink = jnp.zeros((1, Q_GROUP * Q_BLOCK), F32)
            for gi in range(Q_GROUP):
                sink = jnp.where(head_of_lane == gi, sink_ref[j * Q_GROUP + gi], sink)
            top = jnp.maximum(jnp.max(s_loc, axis=0, keepdims=True), jnp.max(s_ctx, axis=0, keepdims=True))
            m_b = jnp.maximum(top.astype(F32), sink).astype(BF16)
            m = m_b.astype(F32)
            p_all = jnp.concatenate([jnp.exp(s_loc - m_b), jnp.exp(s_ctx - m_b)], axis=0)
            vt_all = jnp.concatenate([vt_ref[k_block + t, vs, :] for t in range(KEY_BLOCKS)]
                                     + [vct_ref[t, vs, :] for t in range(n_ctx_blocks)], axis=1)
            out_t = _dot(vt_all, p_all)
            denom = out_t[HEAD_DIM:HEAD_DIM + 1, :] + jnp.exp(sink - m)
            vals = out_t[:HEAD_DIM, :] / denom
            for pair in range(Q_GROUP // 2):
                two_heads = jnp.concatenate([vals[:, (2 * pair) * Q_BLOCK:(2 * pair + 1) * Q_BLOCK],
                                             vals[:, (2 * pair + 1) * Q_BLOCK:(2 * pair + 2) * Q_BLOCK]], axis=0)
                head_groups.setdefault(wi, []).append(two_heads.T.astype(BF16))
        for c in range(N_KV_HEADS):
            project(ATTN_UNROLL - 1, c)
        return carry

    lax.fori_loop(0, ATTN_ROWS // (Q_BLOCK * ATTN_UNROLL), block, 0)


def _attention(sink, q, k2, vt, kc2, vct, w_a, ga):
    b, n, dq = q.shape
    n_ctx, d_out = kc2.shape[1], w_a.shape[1]
    assert n % ATTN_ROWS == 0 and n >= KEY_SPAN and n_ctx % Q_BLOCK == 0 and ATTN_ROWS % (Q_BLOCK * ATTN_UNROLL) == 0
    kern = functools.partial(_attn_kernel, seq=n)
    per_sample = lambda arr: pl.BlockSpec((None,) + arr.shape[1:], lambda bi, si: (bi,) + (0,) * (arr.ndim - 1))
    return pl.pallas_call(
        kern,
        grid=(b, n // ATTN_ROWS),
        in_specs=[
            pl.BlockSpec(memory_space=pltpu.SMEM),
            pl.BlockSpec((None, ATTN_ROWS, dq), lambda bi, si: (bi, si, 0)),
            per_sample(k2), per_sample(vt), per_sample(kc2), per_sample(vct),
            _resident(w_a.shape),
            pl.BlockSpec((None, ATTN_ROWS, d_out), lambda bi, si: (bi, si, 0)),
        ],
        out_specs=pl.BlockSpec((None, ATTN_ROWS, d_out), lambda bi, si: (bi, si, 0)),
        out_shape=jax.ShapeDtypeStruct((b, n, d_out), BF16),
        compiler_params=_params(2),
        name="attention",
    )(sink, q, k2, vt, kc2, vct, w_a, ga)


def _weighted_sum(terms, coefs, eps=1e-9):
    by_magnitude = {}
    for term, coef in zip(terms, coefs):
        if abs(coef) > eps:
            plus, minus = by_magnitude.setdefault(round(abs(coef), 9), ([], []))
            (plus if coef > 0 else minus).append(term)
    total = None
    for magnitude, (plus, minus) in by_magnitude.items():
        part = functools.reduce(lambda a, b: a + b, plus) if plus else None
        if minus:
            neg = functools.reduce(lambda a, b: a + b, minus)
            part = -neg if part is None else part - neg
        if abs(magnitude - 1.0) > eps:
            part = part * magnitude
        total = part if total is None else total + part
    return total


def _fourier_kernel(u_ref, tc_ref, ts_ref, wch_ref, wseq_ref, wf_ref, o_ref, v_scr, y_scr, *, n2):
    gd = FOURIER_GROUP_DIM
    d_out = wf_ref.shape[1]
    row_chunk = 256

    pairs = [(k1,) if (FFT_RADIX - k1) % FFT_RADIX == k1 else (k1, FFT_RADIX - k1)
             for k1 in range(FFT_RADIX // 2 + 1)]
    slot_of = {k1: (2 * idx + pos) % v_scr.shape[0] for idx, pair in enumerate(pairs) for pos, k1 in enumerate(pair)}

    def radix_stage(pair):
        k1 = pair[0]
        turns = [2.0 * np.pi * ((n1 * k1) % FFT_RADIX) / FFT_RADIX for n1 in range(FFT_RADIX)]
        re_coef = [float(np.cos(t)) for t in turns]
        im_coef = [-float(np.sin(t)) for t in turns]
        for g in range(N_FOURIER_GROUPS):
            for r0 in range(0, n2, row_chunk):
                us = [u_ref[n1 * n2 + r0:n1 * n2 + r0 + row_chunk, g * gd:(g + 1) * gd].astype(F32)
                      for n1 in range(FFT_RADIX)]
                er = _weighted_sum(us, re_coef)
                ei = _weighted_sum(us, im_coef)
                for pos, k in enumerate(pair):
                    ek = ei if pos == 0 or ei is None else -ei
                    if k == 0:
                        vr, vi = er, jnp.zeros_like(er)
                    else:
                        c = tc_ref[k - 1, r0:r0 + row_chunk, :]
                        s = ts_ref[k - 1, r0:r0 + row_chunk, :]
                        vr, vi = (er * c, -(er * s)) if ek is None else (er * c + ek * s, ek * c - er * s)
                    v_scr[slot_of[k], r0:r0 + row_chunk, 2 * g * gd:(2 * g + 1) * gd] = vr.astype(BF16)
                    v_scr[slot_of[k], r0:r0 + row_chunk, (2 * g + 1) * gd:(2 * g + 2) * gd] = vi.astype(BF16)

    def channel_stage(k1):
        slot = slot_of[k1]
        for g in range(N_FOURIER_GROUPS):
            y = _dot(v_scr[slot, :, 2 * g * gd:(2 * g + 2) * gd], wch_ref[...])
            y_scr[slot, 0:n2, g * gd:(g + 1) * gd] = y[:, :gd].astype(BF16)
            y_scr[slot, n2:2 * n2, g * gd:(g + 1) * gd] = y[:, gd:].astype(BF16)

    def sequence_stage(k1):
        f = _dot(wseq_ref[...], y_scr[slot_of[k1]]).astype(BF16)
        o_ref[:, k1 * d_out:(k1 + 1) * d_out] = _dot(f, wf_ref[...]).astype(o_ref.dtype)

    radix_stage(pairs[0])
    for idx, pair in enumerate(pairs):
        for pos, k1 in enumerate(pair):
            channel_stage(k1)
            if pos == 0 and idx + 1 < len(pairs):
                radix_stage(pairs[idx + 1])
            sequence_stage(k1)


def _fourier_tables(n):
    n2 = n // FFT_RADIX
    gd = FOURIER_GROUP_DIM
    pos = np.arange(n2, dtype=np.int64)
    k1 = np.arange(1, FFT_RADIX, dtype=np.int64)
    ang = 2.0 * np.pi * ((k1[:, None] * pos[None, :]) % n) / n
    tw_cos = np.broadcast_to(np.cos(ang)[:, :, None], (FFT_RADIX - 1, n2, LANES)).astype(np.float32)
    tw_sin = np.broadcast_to(np.sin(ang)[:, :, None], (FFT_RADIX - 1, n2, LANES)).astype(np.float32)
    ch = np.arange(gd, dtype=np.int64)
    ang_c = 2.0 * np.pi * ((ch[:, None] * ch[None, :]) % gd) / gd
    cc, sc = np.cos(ang_c), np.sin(ang_c)
    w_ch = np.block([[cc, -sc], [sc, cc]]).astype(np.float32)
    kk = np.arange(n2, dtype=np.int64)
    ang_s = 2.0 * np.pi * ((kk[:, None] * kk[None, :]) % n2) / n2
    scale = 1.0 / np.sqrt(float(n) * gd)
    w_seq = (np.concatenate([np.cos(ang_s), np.sin(ang_s)], axis=1) * scale).astype(np.float32)
    return jnp.asarray(tw_cos), jnp.asarray(tw_sin), jnp.asarray(w_ch), jnp.asarray(w_seq)


def _fourier(u, w_f):
    b, n, width = u.shape
    d_out = w_f.shape[1]
    assert n % (FFT_RADIX * 256) == 0 and width == N_FOURIER_GROUPS * FOURIER_GROUP_DIM
    n2 = n // FFT_RADIX
    tw_cos, tw_sin, w_ch, w_seq = _fourier_tables(n)
    kern = functools.partial(_fourier_kernel, n2=n2)
    return pl.pallas_call(
        kern,
        grid=(b,),
        in_specs=[
            pl.BlockSpec((None, n, width), lambda bi: (bi, 0, 0)),
            _resident(tw_cos.shape),
            _resident(tw_sin.shape),
            _resident(w_ch.shape),
            _resident(w_seq.shape),
            _resident(w_f.shape),
        ],
        out_specs=pl.BlockSpec((None, n2, FFT_RADIX * d_out), lambda bi: (bi, 0, 0)),
        out_shape=jax.ShapeDtypeStruct((b, n2, FFT_RADIX * d_out), BF16),
        scratch_shapes=[pltpu.VMEM((4, n2, 2 * width), BF16), pltpu.VMEM((4, 2 * n2, width), BF16)],
        compiler_params=_params(1),
        name="fourier",
    )(u, tw_cos, tw_sin, w_ch.astype(BF16), w_seq.astype(BF16), w_f)


def _rope_tables(n):
    rows = n // GRID_W
    half = HEAD_DIM // 2
    row = jnp.repeat(jnp.arange(rows, dtype=F32), GRID_W)
    col = jnp.tile(jnp.arange(GRID_W, dtype=F32), rows)
    inv_freq = ROPE_BASE ** (-jnp.arange(0, half, 2, dtype=F32) / half)
    ang = jnp.concatenate([row[:, None] * inv_freq, col[:, None] * inv_freq], axis=-1)
    cos, sin = jnp.cos(ang), jnp.sin(ang)
    reps = LANES // HEAD_DIM
    cos_t = jnp.tile(jnp.repeat(cos, 2, axis=-1), (1, reps))
    sin_t = jnp.tile(jnp.stack([-sin, sin], axis=-1).reshape(n, HEAD_DIM), (1, reps))
    return cos_t, sin_t


def kernel(x, c, ctx, c_ctx, w_ada, b_ada, g_ffn1, w_ffn1_in, w_ffn1_out, g_mix, w_in, attn_sink,
           w_attn_branch, w_fourier_branch, w_out, g_ffn2, w_ffn2_in, w_ffn2_out, g_final):
    b, n, d = x.shape
    depth = w_ada.shape[0]
    assert depth == 1, "single-layer problem: the last-layer path of the reference"
    layer = 0
    fourier_width = N_FOURIER_GROUPS * FOURIER_GROUP_DIM

    cond_rows = 8 * ((b + 1 + 7) // 8)
    cond_in = jnp.zeros((cond_rows, d), F32).at[:b].set(c).at[b].set(c_ctx)
    mod = _modulation(cond_in, w_ada[layer], b_ada[layer]).reshape(cond_rows, N_MOD, d)
    mod_lat, mod_ctx = mod[:b], mod[b:b + 1]

    bf = lambda w: w.astype(BF16)
    w_proj = bf(w_in[layer])
    v_off = (N_HEADS + N_KV_HEADS) * HEAD_DIM
    w_vt = _transposed_columns(w_in[layer], v_off, N_KV_HEADS * HEAD_DIM)
    cos_t, sin_t = _rope_tables(n)

    w1_in, w1_out = bf(w_ffn1_in[layer]), bf(w_ffn1_out[layer])
    x1 = _ffn(x, mod_lat, g_ffn1[layer], w1_in, w1_out, base=0, rows=TOKEN_ROWS)
    ctx_flat = ctx.reshape(1, ctx.shape[0] * ctx.shape[1], d)
    ctx_rows = TOKEN_ROWS if ctx_flat.shape[1] % TOKEN_ROWS == 0 else ctx.shape[1]
    ctx1 = _ffn(ctx_flat, mod_ctx, g_ffn1[layer], w1_in, w1_out, base=0, rows=ctx_rows).reshape(ctx.shape)

    q, k2, vt, u, ga, gf = _in_projection(x1, mod_lat, g_mix[layer], w_proj, w_vt, cos_t, sin_t, fourier_width,
                                          rows=TOKEN_ROWS)
    kc2, vct = _ctx_kv(ctx1, mod_ctx, g_mix[layer], w_proj, w_vt)

    ya = _attention(attn_sink[layer], q, k2, vt, kc2, vct, bf(w_attn_branch[layer]), ga)
    fw = _fourier(u, bf(w_fourier_branch[layer]))
    return _mix_ffn(x1, mod_lat, ya, fw, gf, bf(w_out[layer]), g_ffn2[layer], bf(w_ffn2_in[layer]),
                    bf(w_ffn2_out[layer]), g_final, base=6, rows=MIX_ROWS)
```

```python
import functools

import numpy as np
import jax
import jax.numpy as jnp
from jax import lax
from jax.experimental import pallas as pl
from jax.experimental.pallas import tpu as pltpu

N_HEADS = 16
N_KV_HEADS = 4
HEAD_DIM = 64
Q_GROUP = N_HEADS // N_KV_HEADS
WINDOW = 128
GRID_W = 64
N_FOURIER_GROUPS = 4
FOURIER_GROUP_DIM = 128
ROPE_BASE = 10000.0
N_MOD = 9
RMS_EPS = 1e-6
NEG_INF = -1e30

LANES = 128
MXU_DIM = 256
BF16_SUBLANES = 16
VMEM_BYTES = 64 * 1024 * 1024
VMEM_LIMIT_BYTES = 56 * 1024 * 1024
MIX_VMEM_LIMIT_BYTES = VMEM_BYTES - 1024 * 1024

TOKEN_ROWS = 1024
MIX_ROWS = 1024
FF_CHUNK = MXU_DIM
ATTN_ROWS = 1024
Q_BLOCK = 128
ATTN_UNROLL = 8
SCORE_LOOKAHEAD = 2
KEY_BLOCKS = 3
KEY_SPAN = KEY_BLOCKS * Q_BLOCK
VT_ROWS = HEAD_DIM + BF16_SUBLANES
FFT_RADIX = 8

BF16 = jnp.bfloat16
F32 = jnp.float32


def _params(grid_rank, vmem_limit_bytes=VMEM_LIMIT_BYTES):
    return pltpu.CompilerParams(dimension_semantics=("arbitrary",) * grid_rank, vmem_limit_bytes=vmem_limit_bytes)


def _dot(a, b):
    return jnp.dot(a, b, preferred_element_type=F32)


def _dot_nt(a, b):
    return lax.dot_general(a, b, (((1,), (1,)), ((), ())), preferred_element_type=F32)


def _rms(x, g):
    return x * lax.rsqrt(jnp.mean(x * x, axis=-1, keepdims=True) + RMS_EPS) * g


def _adaln(x, g, mod_ref, base):
    shift = mod_ref[base:base + 1, :]
    scale = mod_ref[base + 1:base + 2, :]
    return _rms(x, g) * (1.0 + scale) + shift


def _resident(shape):
    zeros = (0,) * len(shape)
    return pl.BlockSpec(shape, lambda *_: zeros)


def _lane_iota(rows):
    return lax.broadcasted_iota(jnp.int32, (rows, LANES), 1)


def _mod_kernel(c_ref, w_ref, b_ref, o_ref):
    c = c_ref[...]
    cond = (c * jax.nn.sigmoid(c)).astype(BF16)
    o_ref[...] = _dot(cond, w_ref[...].astype(BF16)) + b_ref[...]


def _modulation(cond_in, w_ada, b_ada):
    rows, d = cond_in.shape
    n_out = w_ada.shape[1]
    return pl.pallas_call(
        _mod_kernel,
        grid=(n_out // d,),
        in_specs=[
            _resident((rows, d)),
            pl.BlockSpec((d, d), lambda j: (0, j)),
            pl.BlockSpec((1, d), lambda j: (0, j)),
        ],
        out_specs=pl.BlockSpec((rows, d), lambda j: (0, j)),
        out_shape=jax.ShapeDtypeStruct((rows, n_out), F32),
        compiler_params=_params(1),
        name="modulation",
    )(cond_in, w_ada, b_ada.reshape(1, n_out))


def _macaron_ffn(x, mod_ref, g_ref, win_ref, wout_ref, base):
    d_ff = wout_ref.shape[0]
    h = _adaln(x, g_ref[...], mod_ref, base).astype(BF16)
    acc = jnp.zeros(x.shape, F32)
    for c0 in range(0, d_ff, FF_CHUNK):
        a = _dot(h, win_ref[:, c0:c0 + FF_CHUNK])
        u = _dot(h, win_ref[:, d_ff + c0:d_ff + c0 + FF_CHUNK])
        act = (a * jax.nn.sigmoid(a) * u).astype(BF16)
        acc = acc + _dot(act, wout_ref[c0:c0 + FF_CHUNK, :])
    return x + 0.5 * mod_ref[base + 2:base + 3, :] * acc


def _ffn_kernel(x_ref, mod_ref, g_ref, win_ref, wout_ref, o_ref, *, base):
    o_ref[...] = _macaron_ffn(x_ref[...], mod_ref, g_ref, win_ref, wout_ref, base)


def _mix_ffn_kernel(x_ref, mod_ref, ya_ref, fw_ref, gf_ref, wo_ref, g_ref, win_ref, wout_ref, gfin_ref, o_ref,
                    fw_scr, *, base):
    quarter, d = fw_ref.shape[0], x_ref.shape[1]
    groups = fw_scr.shape[0]
    for k1 in range(FFT_RADIX):
        for grp in range(groups):
            cols = slice(k1 * d + grp * LANES, k1 * d + (grp + 1) * LANES)
            fw_scr[grp, pl.ds(k1, quarter, stride=FFT_RADIX), :] = fw_ref[:, cols].astype(F32)
    fw = jnp.concatenate([fw_scr[grp] for grp in range(groups)], axis=-1)
    y = ya_ref[...].astype(F32) + gf_ref[...].astype(F32) * fw
    x = x_ref[...] + mod_ref[5:6, :] * _dot(y.astype(BF16), wo_ref[...])
    out = _macaron_ffn(x, mod_ref, g_ref, win_ref, wout_ref, base)
    o_ref[...] = _rms(out, gfin_ref[...])


def _ffn(x, mod, g, w_in, w_out, *, base, rows):
    b, n, d = x.shape
    d_ff = w_out.shape[0]
    assert n % rows == 0 and d_ff % FF_CHUNK == 0
    per_batch_mod = mod.shape[0] == b
    mod_index = (lambda bi, si: (bi, 0, 0)) if per_batch_mod else (lambda bi, si: (0, 0, 0))
    return pl.pallas_call(
        functools.partial(_ffn_kernel, base=base),
        grid=(b, n // rows),
        in_specs=[
            pl.BlockSpec((None, rows, d), lambda bi, si: (bi, si, 0)),
            pl.BlockSpec((None, N_MOD, d), mod_index),
            _resident((1, d)),
            _resident((d, 2 * d_ff)),
            _resident((d_ff, d)),
        ],
        out_specs=pl.BlockSpec((None, rows, d), lambda bi, si: (bi, si, 0)),
        out_shape=jax.ShapeDtypeStruct((b, n, d), F32),
        compiler_params=_params(2),
        name="ffn",
    )(x, mod, g.reshape(1, d), w_in, w_out)


def _mix_ffn(x, mod, ya, fw, gf, w_o, g, w_in, w_out, g_final, *, base, rows):
    b, n, d = x.shape
    d_ff = w_out.shape[0]
    assert n % rows == 0 and rows % FFT_RADIX == 0 and d_ff % FF_CHUNK == 0 and fw.shape[2] == FFT_RADIX * d
    tok = lambda width: pl.BlockSpec((None, rows, width), lambda bi, si: (bi, si, 0))
    return pl.pallas_call(
        functools.partial(_mix_ffn_kernel, base=base),
        grid=(b, n // rows),
        in_specs=[
            tok(d),
            pl.BlockSpec((None, N_MOD, d), lambda bi, si: (bi, 0, 0)),
            tok(d),
            pl.BlockSpec((None, rows // FFT_RADIX, FFT_RADIX * d), lambda bi, si: (bi, si, 0)),
            tok(d),
            _resident(w_o.shape),
            _resident((1, d)),
            _resident((d, 2 * d_ff)),
            _resident((d_ff, d)),
            _resident((1, d)),
        ],
        out_specs=tok(d),
        out_shape=jax.ShapeDtypeStruct((b, n, d), F32),
        scratch_shapes=[pltpu.VMEM((d // LANES, rows, LANES), F32)],
        compiler_params=_params(2, vmem_limit_bytes=MIX_VMEM_LIMIT_BYTES),
        name="mix_ffn_final",
    )(x, mod, ya, fw, gf, w_o, g.reshape(1, d), w_in, w_out, g_final.reshape(1, d))


def _rope(p, cos, sin, even_lane):
    partner = jnp.where(even_lane, pltpu.roll(p, LANES - 1, 1), pltpu.roll(p, 1, 1))
    return p * cos + partner * sin


def _duplicate_heads(pair, low):
    swapped = pltpu.roll(pair, HEAD_DIM, 1)
    return jnp.where(low, pair, swapped), jnp.where(low, swapped, pair)


def _lane_groups(h, w_ref, col, width):
    for c0 in range(0, width, MXU_DIM):
        p = _dot(h, w_ref[:, col + c0:col + c0 + MXU_DIM])
        for half in range(MXU_DIM // LANES):
            yield c0 // LANES + half, p[:, half * LANES:(half + 1) * LANES]


def _store_k(k_ref, h, w_ref, col, width, low, rope):
    for grp, p in _lane_groups(h, w_ref, col, width):
        first, second = _duplicate_heads(rope(p), low)
        k_ref[:, 2 * grp * LANES:(2 * grp + 1) * LANES] = first.astype(k_ref.dtype)
        k_ref[:, (2 * grp + 1) * LANES:(2 * grp + 2) * LANES] = second.astype(k_ref.dtype)


def _store_vt(vt_ref, h, wvt_ref):
    vt = _dot_nt(wvt_ref[...], h)
    ones = jnp.ones((VT_ROWS - HEAD_DIM, Q_BLOCK), vt_ref.dtype)
    for t in range(vt_ref.shape[0]):
        for j in range(N_KV_HEADS):
            vt_ref[t, j * VT_ROWS:j * VT_ROWS + HEAD_DIM, :] = (
                vt[j * HEAD_DIM:(j + 1) * HEAD_DIM, t * Q_BLOCK:(t + 1) * Q_BLOCK].astype(vt_ref.dtype))
            vt_ref[t, j * VT_ROWS + HEAD_DIM:(j + 1) * VT_ROWS, :] = ones


def _inproj_kernel(x_ref, mod_ref, g_ref, w_ref, wvt_ref, cos_ref, sin_ref,
                   q_ref, k_ref, vt_ref, u_ref, ga_ref, gf_ref, *, offsets):
    h = _adaln(x_ref[...], g_ref[...], mod_ref, 3).astype(BF16)
    cos = cos_ref[...]
    sin = sin_ref[...]
    lane = _lane_iota(h.shape[0])
    even_lane = (lane % 2) == 0
    low = lane < HEAD_DIM
    q_scale = HEAD_DIM ** -0.5
    rope = functools.partial(_rope, cos=cos, sin=sin, even_lane=even_lane)
    q_off, k_off, u_off, ga_off, gf_off = offsets

    for grp, p in _lane_groups(h, w_ref, q_off, q_ref.shape[1]):
        q_ref[:, grp * LANES:(grp + 1) * LANES] = (rope(p) * q_scale).astype(q_ref.dtype)
    _store_k(k_ref, h, w_ref, k_off, k_ref.shape[1] // 2, low, rope)
    _store_vt(vt_ref, h, wvt_ref)
    for grp, p in _lane_groups(h, w_ref, u_off, u_ref.shape[1]):
        u_ref[:, grp * LANES:(grp + 1) * LANES] = p.astype(u_ref.dtype)
    for ref, off in ((ga_ref, ga_off), (gf_ref, gf_off)):
        for grp, p in _lane_groups(h, w_ref, off, ref.shape[1]):
            ref[:, grp * LANES:(grp + 1) * LANES] = jax.nn.sigmoid(p).astype(ref.dtype)


def _in_projection(x, mod, g, w, w_vt, cos_t, sin_t, fourier_width, *, rows):
    b, n, d = x.shape
    q_w, kv_w = N_HEADS * HEAD_DIM, N_KV_HEADS * HEAD_DIM
    assert n % rows == 0 and rows % Q_BLOCK == 0 and kv_w % MXU_DIM == 0 and fourier_width % MXU_DIM == 0
    u_off = q_w + 2 * kv_w
    offsets = (0, q_w, u_off, u_off + fourier_width, u_off + fourier_width + d)
    kern = functools.partial(_inproj_kernel, offsets=offsets)
    tok = lambda width: pl.BlockSpec((None, rows, width), lambda si, bi: (bi, si, 0))
    key_blocks = rows // Q_BLOCK
    vt_rows = N_KV_HEADS * VT_ROWS
    out_widths = (q_w, 2 * kv_w, None, fourier_width, d, d)
    out_specs = [tok(width) if width else
                 pl.BlockSpec((None, key_blocks, vt_rows, Q_BLOCK), lambda si, bi: (bi, si, 0, 0))
                 for width in out_widths]
    out_shape = [jax.ShapeDtypeStruct((b, n, width), BF16) if width else
                 jax.ShapeDtypeStruct((b, n // Q_BLOCK, vt_rows, Q_BLOCK), BF16)
                 for width in out_widths]
    return pl.pallas_call(
        kern,
        grid=(n // rows, b),
        in_specs=[
            tok(d),
            pl.BlockSpec((None, N_MOD, d), lambda si, bi: (bi, 0, 0)),
            _resident((1, d)),
            _resident(w.shape),
            _resident(w_vt.shape),
            pl.BlockSpec((rows, LANES), lambda si, bi: (si, 0)),
            pl.BlockSpec((rows, LANES), lambda si, bi: (si, 0)),
        ],
        out_specs=out_specs,
        out_shape=out_shape,
        compiler_params=_params(2),
        name="in_projection",
    )(x, mod, g.reshape(1, d), w, w_vt, cos_t, sin_t)


def _transpose_kernel(w_ref, o_ref):
    o_ref[...] = w_ref[...].T.astype(o_ref.dtype)


def _transposed_columns(w, col, width):
    rows = w.shape[0]
    assert col % width == 0
    return pl.pallas_call(
        _transpose_kernel,
        grid=(1,),
        in_specs=[pl.BlockSpec((rows, width), lambda i: (0, col // width))],
        out_specs=pl.BlockSpec((width, rows), lambda i: (0, 0)),
        out_shape=jax.ShapeDtypeStruct((width, rows), BF16),
        compiler_params=_params(1),
        name="transpose_wv",
    )(w)


def _ctx_kv_kernel(x_ref, mod_ref, g_ref, w_ref, wvt_ref, k_ref, vt_ref, *, k_off):
    h = _adaln(x_ref[...], g_ref[...], mod_ref, 3).astype(BF16)
    low = _lane_iota(h.shape[0]) < HEAD_DIM
    _store_k(k_ref, h, w_ref, k_off, k_ref.shape[1] // 2, low, lambda p: p)
    _store_vt(vt_ref, h, wvt_ref)


def _ctx_kv(ctx, mod_ctx, g, w, w_vt):
    b, n, d = ctx.shape
    kv_w = N_KV_HEADS * HEAD_DIM
    vt_rows = N_KV_HEADS * VT_ROWS
    assert n % Q_BLOCK == 0
    kern = functools.partial(_ctx_kv_kernel, k_off=N_HEADS * HEAD_DIM)
    return pl.pallas_call(
        kern,
        grid=(b,),
        in_specs=[pl.BlockSpec((None, n, d), lambda bi: (bi, 0, 0)),
                  pl.BlockSpec((None, N_MOD, d), lambda bi: (0, 0, 0)),
                  _resident((1, d)), _resident(w.shape), _resident(w_vt.shape)],
        out_specs=[pl.BlockSpec((None, n, 2 * kv_w), lambda bi: (bi, 0, 0)),
                   pl.BlockSpec((None, n // Q_BLOCK, vt_rows, Q_BLOCK), lambda bi: (bi, 0, 0, 0))],
        out_shape=[jax.ShapeDtypeStruct((b, n, 2 * kv_w), BF16),
                   jax.ShapeDtypeStruct((b, n // Q_BLOCK, vt_rows, Q_BLOCK), BF16)],
        compiler_params=_params(1),
        name="ctx_kv",
    )(ctx, mod_ctx, g.reshape(1, d), w, w_vt)


def _attn_kernel(sink_ref, q_ref, k_ref, vt_ref, kc_ref, vct_ref, wa_ref, ga_ref, o_ref, *, seq):
    tile_start = pl.program_id(1) * ATTN_ROWS
    lane = _lane_iota(Q_BLOCK)
    keep_low = (lane < HEAD_DIM).astype(BF16)
    keep_high = 1.0 - keep_low
    key_row = lax.broadcasted_iota(jnp.int32, (KEY_SPAN, Q_BLOCK), 0)
    query_col = lax.broadcasted_iota(jnp.int32, (KEY_SPAN, Q_BLOCK), 1)
    head_of_lane = lax.broadcasted_iota(jnp.int32, (1, Q_GROUP * Q_BLOCK), 1) // Q_BLOCK
    n_ctx_blocks = vct_ref.shape[0]

    def window(sb):
        q_start = tile_start + sb * Q_BLOCK
        k_start = pl.multiple_of(jnp.clip(q_start - Q_BLOCK, 0, seq - KEY_SPAN), Q_BLOCK)
        dist = key_row - query_col + (k_start - q_start)
        bias = jnp.where(jnp.abs(dist) <= WINDOW, 0.0, NEG_INF).astype(F32)
        return pl.multiple_of(sb * Q_BLOCK, Q_BLOCK), k_start, bias

    def scores(win, j):
        r0, k_start, bias = win
        ks = slice(j * LANES, (j + 1) * LANES)
        qa = q_ref[pl.ds(r0, Q_BLOCK), 2 * j * LANES:(2 * j + 1) * LANES]
        qb = q_ref[pl.ds(r0, Q_BLOCK), (2 * j + 1) * LANES:(2 * j + 2) * LANES]
        q4 = jnp.concatenate([qa * keep_low, qa * keep_high, qb * keep_low, qb * keep_high], axis=0)
        keys = jnp.concatenate([k_ref[pl.ds(k_start, KEY_SPAN), ks], kc_ref[:, ks]], axis=0)
        s_all = _dot_nt(keys, q4)
        s_loc = jnp.concatenate([s_all[:KEY_SPAN, g * Q_BLOCK:(g + 1) * Q_BLOCK] + bias for g in range(Q_GROUP)],
                                axis=1)
        return s_loc.astype(BF16), s_all[KEY_SPAN:].astype(BF16)

    def block(step, carry):
        wins = [window(step * ATTN_UNROLL + i) for i in range(ATTN_UNROLL)]
        units = [(wi, j) for wi in range(ATTN_UNROLL) for j in range(N_KV_HEADS)]
        head_groups = {}
        piece = wa_ref.shape[1] // N_KV_HEADS

        def project(wi, c):
            attn_rows = jnp.concatenate(head_groups[wi], axis=1)
            rows, cols = pl.ds(wins[wi][0], Q_BLOCK), slice(c * piece, (c + 1) * piece)
            gated = ga_ref[rows, cols].astype(F32) * _dot(attn_rows, wa_ref[:, cols])
            o_ref[rows, cols] = gated.astype(o_ref.dtype)

        pending = [scores(wins[wi], j) for wi, j in units[:SCORE_LOOKAHEAD]]
        for idx, (wi, j) in enumerate(units):
            k_block = wins[wi][1] // Q_BLOCK
            s_loc, s_ctx = pending.pop(0)
            if idx + SCORE_LOOKAHEAD < len(units):
                nwi, nj = units[idx + SCORE_LOOKAHEAD]
                pending.append(scores(wins[nwi], nj))
            if wi > 0:
                project(wi - 1, j)
            vs = slice(j * VT_ROWS, (j + 1) * VT_ROWS)
            sink = jnp.zeros((1, Q_GROUP * Q_BLOCK), F32)
            for gi in range(Q_GROUP):
                sink = jnp.where(head_of_lane == gi, sink_ref[j * Q_GROUP + gi], sink)
            top = jnp.maximum(jnp.max(s_loc, axis=0, keepdims=True), jnp.max(s_ctx, axis=0, keepdims=True))
            m_b = jnp.maximum(top.astype(F32), sink).astype(BF16)
            m = m_b.astype(F32)
            p_all = jnp.concatenate([jnp.exp(s_loc - m_b), jnp.exp(s_ctx - m_b)], axis=0)
            vt_all = jnp.concatenate([vt_ref[k_block + t, vs, :] for t in range(KEY_BLOCKS)]
                                     + [vct_ref[t, vs, :] for t in range(n_ctx_blocks)], axis=1)
            out_t = _dot(vt_all, p_all)
            denom = out_t[HEAD_DIM:HEAD_DIM + 1, :] + jnp.exp(sink - m)
            vals = out_t[:HEAD_DIM, :] / denom
            for pair in range(Q_GROUP // 2):
                two_heads = jnp.concatenate([vals[:, (2 * pair) * Q_BLOCK:(2 * pair + 1) * Q_BLOCK],
                                             vals[:, (2 * pair + 1) * Q_BLOCK:(2 * pair + 2) * Q_BLOCK]], axis=0)
                head_groups.setdefault(wi, []).append(two_heads.T.astype(BF16))
        for c in range(N_KV_HEADS):
            project(ATTN_UNROLL - 1, c)
        return carry

    lax.fori_loop(0, ATTN_ROWS // (Q_BLOCK * ATTN_UNROLL), block, 0)


def _attention(sink, q, k2, vt, kc2, vct, w_a, ga):
    b, n, dq = q.shape
    n_ctx, d_out = kc2.shape[1], w_a.shape[1]
    assert n % ATTN_ROWS == 0 and n >= KEY_SPAN and n_ctx % Q_BLOCK == 0 and ATTN_ROWS % (Q_BLOCK * ATTN_UNROLL) == 0
    kern = functools.partial(_attn_kernel, seq=n)
    per_sample = lambda arr: pl.BlockSpec((None,) + arr.shape[1:], lambda bi, si: (bi,) + (0,) * (arr.ndim - 1))
    return pl.pallas_call(
        kern,
        grid=(b, n // ATTN_ROWS),
        in_specs=[
            pl.BlockSpec(memory_space=pltpu.SMEM),
            pl.BlockSpec((None, ATTN_ROWS, dq), lambda bi, si: (bi, si, 0)),
            per_sample(k2), per_sample(vt), per_sample(kc2), per_sample(vct),
            _resident(w_a.shape),
            pl.BlockSpec((None, ATTN_ROWS, d_out), lambda bi, si: (bi, si, 0)),
        ],
        out_specs=pl.BlockSpec((None, ATTN_ROWS, d_out), lambda bi, si: (bi, si, 0)),
        out_shape=jax.ShapeDtypeStruct((b, n, d_out), BF16),
        compiler_params=_params(2),
        name="attention",
    )(sink, q, k2, vt, kc2, vct, w_a, ga)


def _weighted_sum(terms, coefs, eps=1e-9):
    by_magnitude = {}
    for term, coef in zip(terms, coefs):
        if abs(coef) > eps:
            plus, minus = by_magnitude.setdefault(round(abs(coef), 9), ([], []))
            (plus if coef > 0 else minus).append(term)
    total = None
    for magnitude, (plus, minus) in by_magnitude.items():
        part = functools.reduce(lambda a, b: a + b, plus) if plus else None
        if minus:
            neg = functools.reduce(lambda a, b: a + b, minus)
            part = -neg if part is None else part - neg
        if abs(magnitude - 1.0) > eps:
            part = part * magnitude
        total = part if total is None else total + part
    return total


def _fourier_kernel(u_ref, tc_ref, ts_ref, wch_ref, wseq_ref, wf_ref, o_ref, v_scr, y_scr, *, n2):
    gd = FOURIER_GROUP_DIM
    d_out = wf_ref.shape[1]
    row_chunk = 256

    pairs = [(k1,) if (FFT_RADIX - k1) % FFT_RADIX == k1 else (k1, FFT_RADIX - k1)
             for k1 in range(FFT_RADIX // 2 + 1)]
    slot_of = {k1: (2 * idx + pos) % v_scr.shape[0] for idx, pair in enumerate(pairs) for pos, k1 in enumerate(pair)}

    def radix_stage(pair):
        k1 = pair[0]
        turns = [2.0 * np.pi * ((n1 * k1) % FFT_RADIX) / FFT_RADIX for n1 in range(FFT_RADIX)]
        re_coef = [float(np.cos(t)) for t in turns]
        im_coef = [-float(np.sin(t)) for t in turns]
        for g in range(N_FOURIER_GROUPS):
            for r0 in range(0, n2, row_chunk):
                us = [u_ref[n1 * n2 + r0:n1 * n2 + r0 + row_chunk, g * gd:(g + 1) * gd].astype(F32)
                      for n1 in range(FFT_RADIX)]
                er = _weighted_sum(us, re_coef)
                ei = _weighted_sum(us, im_coef)
                for pos, k in enumerate(pair):
                    ek = ei if pos == 0 or ei is None else -ei
                    if k == 0:
                        vr, vi = er, jnp.zeros_like(er)
                    else:
                        c = tc_ref[k - 1, r0:r0 + row_chunk, :]
                        s = ts_ref[k - 1, r0:r0 + row_chunk, :]
                        vr, vi = (er * c, -(er * s)) if ek is None else (er * c + ek * s, ek * c - er * s)
                    v_scr[slot_of[k], r0:r0 + row_chunk, 2 * g * gd:(2 * g + 1) * gd] = vr.astype(BF16)
                    v_scr[slot_of[k], r0:r0 + row_chunk, (2 * g + 1) * gd:(2 * g + 2) * gd] = vi.astype(BF16)

    def channel_stage(k1):
        slot = slot_of[k1]
        for g in range(N_FOURIER_GROUPS):
            y = _dot(v_scr[slot, :, 2 * g * gd:(2 * g + 2) * gd], wch_ref[...])
            y_scr[slot, 0:n2, g * gd:(g + 1) * gd] = y[:, :gd].astype(BF16)
            y_scr[slot, n2:2 * n2, g * gd:(g + 1) * gd] = y[:, gd:].astype(BF16)

    def sequence_stage(k1):
        f = _dot(wseq_ref[...], y_scr[slot_of[k1]]).astype(BF16)
        o_ref[:, k1 * d_out:(k1 + 1) * d_out] = _dot(f, wf_ref[...]).astype(o_ref.dtype)

    radix_stage(pairs[0])
    for idx, pair in enumerate(pairs):
        for pos, k1 in enumerate(pair):
            channel_stage(k1)
            if pos == 0 and idx + 1 < len(pairs):
                radix_stage(pairs[idx + 1])
            sequence_stage(k1)


def _fourier_tables(n):
    n2 = n // FFT_RADIX
    gd = FOURIER_GROUP_DIM
    pos = np.arange(n2, dtype=np.int64)
    k1 = np.arange(1, FFT_RADIX, dtype=np.int64)
    ang = 2.0 * np.pi * ((k1[:, None] * pos[None, :]) % n) / n
    tw_cos = np.broadcast_to(np.cos(ang)[:, :, None], (FFT_RADIX - 1, n2, LANES)).astype(np.float32)
    tw_sin = np.broadcast_to(np.sin(ang)[:, :, None], (FFT_RADIX - 1, n2, LANES)).astype(np.float32)
    ch = np.arange(gd, dtype=np.int64)
    ang_c = 2.0 * np.pi * ((ch[:, None] * ch[None, :]) % gd) / gd
    cc, sc = np.cos(ang_c), np.sin(ang_c)
    w_ch = np.block([[cc, -sc], [sc, cc]]).astype(np.float32)
    kk = np.arange(n2, dtype=np.int64)
    ang_s = 2.0 * np.pi * ((kk[:, None] * kk[None, :]) % n2) / n2
    scale = 1.0 / np.sqrt(float(n) * gd)
    w_seq = (np.concatenate([np.cos(ang_s), np.sin(ang_s)], axis=1) * scale).astype(np.float32)
    return jnp.asarray(tw_cos), jnp.asarray(tw_sin), jnp.asarray(w_ch), jnp.asarray(w_seq)


def _fourier(u, w_f):
    b, n, width = u.shape
    d_out = w_f.shape[1]
    assert n % (FFT_RADIX * 256) == 0 and width == N_FOURIER_GROUPS * FOURIER_GROUP_DIM
    n2 = n // FFT_RADIX
    tw_cos, tw_sin, w_ch, w_seq = _fourier_tables(n)
    kern = functools.partial(_fourier_kernel, n2=n2)
    return pl.pallas_call(
        kern,
        grid=(b,),
        in_specs=[
            pl.BlockSpec((None, n, width), lambda bi: (bi, 0, 0)),
            _resident(tw_cos.shape),
            _resident(tw_sin.shape),
            _resident(w_ch.shape),
            _resident(w_seq.shape),
            _resident(w_f.shape),
        ],
        out_specs=pl.BlockSpec((None, n2, FFT_RADIX * d_out), lambda bi: (bi, 0, 0)),
        out_shape=jax.ShapeDtypeStruct((b, n2, FFT_RADIX * d_out), BF16),
        scratch_shapes=[pltpu.VMEM((4, n2, 2 * width), BF16), pltpu.VMEM((4, 2 * n2, width), BF16)],
        compiler_params=_params(1),
        name="fourier",
    )(u, tw_cos, tw_sin, w_ch.astype(BF16), w_seq.astype(BF16), w_f)


def _rope_tables(n):
    rows = n // GRID_W
    half = HEAD_DIM // 2
    row = jnp.repeat(jnp.arange(rows, dtype=F32), GRID_W)
    col = jnp.tile(jnp.arange(GRID_W, dtype=F32), rows)
    inv_freq = ROPE_BASE ** (-jnp.arange(0, half, 2, dtype=F32) / half)
    ang = jnp.concatenate([row[:, None] * inv_freq, col[:, None] * inv_freq], axis=-1)
    cos, sin = jnp.cos(ang), jnp.sin(ang)
    reps = LANES // HEAD_DIM
    cos_t = jnp.tile(jnp.repeat(cos, 2, axis=-1), (1, reps))
    sin_t = jnp.tile(jnp.stack([-sin, sin], axis=-1).reshape(n, HEAD_DIM), (1, reps))
    return cos_t, sin_t


def kernel(x, c, ctx, c_ctx, w_ada, b_ada, g_ffn1, w_ffn1_in, w_ffn1_out, g_mix, w_in, attn_sink,
           w_attn_branch, w_fourier_branch, w_out, g_ffn2, w_ffn2_in, w_ffn2_out, g_final):
    b, n, d = x.shape
    depth = w_ada.shape[0]
    assert depth == 1, "single-layer problem: the last-layer path of the reference"
    layer = 0
    fourier_width = N_FOURIER_GROUPS * FOURIER_GROUP_DIM

    cond_rows = 8 * ((b + 1 + 7) // 8)
    cond_in = jnp.zeros((cond_rows, d), F32).at[:b].set(c).at[b].set(c_ctx)
    mod = _modulation(cond_in, w_ada[layer], b_ada[layer]).reshape(cond_rows, N_MOD, d)
    mod_lat, mod_ctx = mod[:b], mod[b:b + 1]

    bf = lambda w: w.astype(BF16)
    w_proj = bf(w_in[layer])
    v_off = (N_HEADS + N_KV_HEADS) * HEAD_DIM
    w_vt = _transposed_columns(w_in[layer], v_off, N_KV_HEADS * HEAD_DIM)
    cos_t, sin_t = _rope_tables(n)

    w1_in, w1_out = bf(w_ffn1_in[layer]), bf(w_ffn1_out[layer])
    x1 = _ffn(x, mod_lat, g_ffn1[layer], w1_in, w1_out, base=0, rows=TOKEN_ROWS)
    ctx_flat = ctx.reshape(1, ctx.shape[0] * ctx.shape[1], d)
    ctx_rows = TOKEN_ROWS if ctx_flat.shape[1] % TOKEN_ROWS == 0 else ctx.shape[1]
    ctx1 = _ffn(ctx_flat, mod_ctx, g_ffn1[layer], w1_in, w1_out, base=0, rows=ctx_rows).reshape(ctx.shape)

    q, k2, vt, u, ga, gf = _in_projection(x1, mod_lat, g_mix[layer], w_proj, w_vt, cos_t, sin_t, fourier_width,
                                          rows=TOKEN_ROWS)
    kc2, vct = _ctx_kv(ctx1, mod_ctx, g_mix[layer], w_proj, w_vt)

    ya = _attention(attn_sink[layer], q, k2, vt, kc2, vct, bf(w_attn_branch[layer]), ga)
    fw = _fourier(u, bf(w_fourier_branch[layer]))
    return _mix_ffn(x1, mod_lat, ya, fw, gf, bf(w_out[layer]), g_ffn2[layer], bf(w_ffn2_in[layer]),
                    bf(w_ffn2_out[layer]), g_final, base=6, rows=MIX_ROWS)
```

```python
import functools

import numpy as np
import jax
import jax.numpy as jnp
from jax import lax
from jax.experimental import pallas as pl
from jax.experimental.pallas import tpu as pltpu

N_HEADS = 16
N_KV_HEADS = 4
HEAD_DIM = 64
Q_GROUP = N_HEADS // N_KV_HEADS
WINDOW = 128
GRID_W = 64
N_FOURIER_GROUPS = 4
FOURIER_GROUP_DIM = 128
ROPE_BASE = 10000.0
N_MOD = 9
RMS_EPS = 1e-6
NEG_INF = -1e30

LANES = 128
MXU_DIM = 256
BF16_SUBLANES = 16
VMEM_LIMIT_BYTES = 56 * 1024 * 1024

TOKEN_ROWS = 1024
MIX_ROWS = 512
FF_CHUNK = MXU_DIM
ATTN_ROWS = 1024
Q_BLOCK = 128
ATTN_UNROLL = 8
SCORE_LOOKAHEAD = 2
KEY_BLOCKS = 3
KEY_SPAN = KEY_BLOCKS * Q_BLOCK
VT_ROWS = HEAD_DIM + BF16_SUBLANES
FFT_RADIX = 8

BF16 = jnp.bfloat16
F32 = jnp.float32


def _params(grid_rank):
    return pltpu.CompilerParams(dimension_semantics=("arbitrary",) * grid_rank, vmem_limit_bytes=VMEM_LIMIT_BYTES)


def _dot(a, b):
    return jnp.dot(a, b, preferred_element_type=F32)


def _dot_nt(a, b):
    return lax.dot_general(a, b, (((1,), (1,)), ((), ())), preferred_element_type=F32)


def _rms(x, g):
    return x * lax.rsqrt(jnp.mean(x * x, axis=-1, keepdims=True) + RMS_EPS) * g


def _adaln(x, g, mod_ref, base):
    shift = mod_ref[base:base + 1, :]
    scale = mod_ref[base + 1:base + 2, :]
    return _rms(x, g) * (1.0 + scale) + shift


def _resident(shape):
    zeros = (0,) * len(shape)
    return pl.BlockSpec(shape, lambda *_: zeros)


def _lane_iota(rows):
    return lax.broadcasted_iota(jnp.int32, (rows, LANES), 1)


def _mod_kernel(c_ref, w_ref, b_ref, o_ref):
    c = c_ref[...]
    cond = (c * jax.nn.sigmoid(c)).astype(BF16)
    o_ref[...] = _dot(cond, w_ref[...].astype(BF16)) + b_ref[...]


def _modulation(cond_in, w_ada, b_ada):
    rows, d = cond_in.shape
    n_out = w_ada.shape[1]
    return pl.pallas_call(
        _mod_kernel,
        grid=(n_out // d,),
        in_specs=[
            _resident((rows, d)),
            pl.BlockSpec((d, d), lambda j: (0, j)),
            pl.BlockSpec((1, d), lambda j: (0, j)),
        ],
        out_specs=pl.BlockSpec((rows, d), lambda j: (0, j)),
        out_shape=jax.ShapeDtypeStruct((rows, n_out), F32),
        compiler_params=_params(1),
        name="modulation",
    )(cond_in, w_ada, b_ada.reshape(1, n_out))


def _macaron_ffn(x, mod_ref, g_ref, win_ref, wout_ref, base):
    d_ff = wout_ref.shape[0]
    h = _adaln(x, g_ref[...], mod_ref, base).astype(BF16)
    acc = jnp.zeros(x.shape, F32)
    for c0 in range(0, d_ff, FF_CHUNK):
        a = _dot(h, win_ref[:, c0:c0 + FF_CHUNK])
        u = _dot(h, win_ref[:, d_ff + c0:d_ff + c0 + FF_CHUNK])
        act = (a * jax.nn.sigmoid(a) * u).astype(BF16)
        acc = acc + _dot(act, wout_ref[c0:c0 + FF_CHUNK, :])
    return x + 0.5 * mod_ref[base + 2:base + 3, :] * acc


def _ffn_kernel(x_ref, mod_ref, g_ref, win_ref, wout_ref, o_ref, *, base):
    o_ref[...] = _macaron_ffn(x_ref[...], mod_ref, g_ref, win_ref, wout_ref, base)


def _mix_ffn_kernel(x_ref, mod_ref, ya_ref, fw_ref, gf_ref, wo_ref, g_ref, win_ref, wout_ref, gfin_ref, o_ref,
                    fw_scr, *, base):
    quarter, d = fw_ref.shape[0], x_ref.shape[1]
    groups = fw_scr.shape[0]
    for k1 in range(FFT_RADIX):
        for grp in range(groups):
            cols = slice(k1 * d + grp * LANES, k1 * d + (grp + 1) * LANES)
            fw_scr[grp, pl.ds(k1, quarter, stride=FFT_RADIX), :] = fw_ref[:, cols].astype(F32)
    fw = jnp.concatenate([fw_scr[grp] for grp in range(groups)], axis=-1)
    y = ya_ref[...].astype(F32) + gf_ref[...].astype(F32) * fw
    x = x_ref[...] + mod_ref[5:6, :] * _dot(y.astype(BF16), wo_ref[...])
    out = _macaron_ffn(x, mod_ref, g_ref, win_ref, wout_ref, base)
    o_ref[...] = _rms(out, gfin_ref[...])


def _ffn(x, mod, g, w_in, w_out, *, base, rows):
    b, n, d = x.shape
    d_ff = w_out.shape[0]
    assert n % rows == 0 and d_ff % FF_CHUNK == 0
    per_batch_mod = mod.shape[0] == b
    mod_index = (lambda bi, si: (bi, 0, 0)) if per_batch_mod else (lambda bi, si: (0, 0, 0))
    return pl.pallas_call(
        functools.partial(_ffn_kernel, base=base),
        grid=(b, n // rows),
        in_specs=[
            pl.BlockSpec((None, rows, d), lambda bi, si: (bi, si, 0)),
            pl.BlockSpec((None, N_MOD, d), mod_index),
            _resident((1, d)),
            _resident((d, 2 * d_ff)),
            _resident((d_ff, d)),
        ],
        out_specs=pl.BlockSpec((None, rows, d), lambda bi, si: (bi, si, 0)),
        out_shape=jax.ShapeDtypeStruct((b, n, d), F32),
        compiler_params=_params(2),
        name="ffn",
    )(x, mod, g.reshape(1, d), w_in, w_out)


def _mix_ffn(x, mod, ya, fw, gf, w_o, g, w_in, w_out, g_final, *, base, rows):
    b, n, d = x.shape
    d_ff = w_out.shape[0]
    assert n % rows == 0 and rows % FFT_RADIX == 0 and d_ff % FF_CHUNK == 0 and fw.shape[2] == FFT_RADIX * d
    tok = lambda width: pl.BlockSpec((None, rows, width), lambda bi, si: (bi, si, 0))
    return pl.pallas_call(
        functools.partial(_mix_ffn_kernel, base=base),
        grid=(b, n // rows),
        in_specs=[
            tok(d),
            pl.BlockSpec((None, N_MOD, d), lambda bi, si: (bi, 0, 0)),
            tok(d),
            pl.BlockSpec((None, rows // FFT_RADIX, FFT_RADIX * d), lambda bi, si: (bi, si, 0)),
            tok(d),
            _resident(w_o.shape),
            _resident((1, d)),
            _resident((d, 2 * d_ff)),
            _resident((d_ff, d)),
            _resident((1, d)),
        ],
        out_specs=tok(d),
        out_shape=jax.ShapeDtypeStruct((b, n, d), F32),
        scratch_shapes=[pltpu.VMEM((d // LANES, rows, LANES), F32)],
        compiler_params=_params(2),
        name="mix_ffn_final",
    )(x, mod, ya, fw, gf, w_o, g.reshape(1, d), w_in, w_out, g_final.reshape(1, d))


def _rope(p, cos, sin, even_lane):
    partner = jnp.where(even_lane, pltpu.roll(p, LANES - 1, 1), pltpu.roll(p, 1, 1))
    return p * cos + partner * sin


def _duplicate_heads(pair, low):
    swapped = pltpu.roll(pair, HEAD_DIM, 1)
    return jnp.where(low, pair, swapped), jnp.where(low, swapped, pair)


def _lane_groups(h, w_ref, col, width):
    for c0 in range(0, width, MXU_DIM):
        p = _dot(h, w_ref[:, col + c0:col + c0 + MXU_DIM])
        for half in range(MXU_DIM // LANES):
            yield c0 // LANES + half, p[:, half * LANES:(half + 1) * LANES]


def _store_k(k_ref, h, w_ref, col, width, low, rope):
    for grp, p in _lane_groups(h, w_ref, col, width):
        first, second = _duplicate_heads(rope(p), low)
        k_ref[:, 2 * grp * LANES:(2 * grp + 1) * LANES] = first.astype(k_ref.dtype)
        k_ref[:, (2 * grp + 1) * LANES:(2 * grp + 2) * LANES] = second.astype(k_ref.dtype)


def _store_vt(vt_ref, h, wvt_ref):
    vt = _dot_nt(wvt_ref[...], h)
    ones = jnp.ones((VT_ROWS - HEAD_DIM, Q_BLOCK), vt_ref.dtype)
    for t in range(vt_ref.shape[0]):
        for j in range(N_KV_HEADS):
            vt_ref[t, j * VT_ROWS:j * VT_ROWS + HEAD_DIM, :] = (
                vt[j * HEAD_DIM:(j + 1) * HEAD_DIM, t * Q_BLOCK:(t + 1) * Q_BLOCK].astype(vt_ref.dtype))
            vt_ref[t, j * VT_ROWS + HEAD_DIM:(j + 1) * VT_ROWS, :] = ones


def _ffn_inproj_kernel(x_ref, mod_ref, g1_ref, win_ref, wout_ref, g_ref, w_ref, wvt_ref, cos_ref, sin_ref,
                       x1_ref, q_ref, k_ref, vt_ref, u_ref, ga_ref, gf_ref, *, offsets):
    x1 = _macaron_ffn(x_ref[...], mod_ref, g1_ref, win_ref, wout_ref, 0)
    x1_ref[...] = x1
    h = _adaln(x1, g_ref[...], mod_ref, 3).astype(BF16)
    _project(h, w_ref, wvt_ref, cos_ref, sin_ref, q_ref, k_ref, vt_ref, u_ref, ga_ref, gf_ref, offsets)


def _inproj_kernel(x_ref, mod_ref, g_ref, w_ref, wvt_ref, cos_ref, sin_ref,
                   q_ref, k_ref, vt_ref, u_ref, ga_ref, gf_ref, *, offsets):
    h = _adaln(x_ref[...], g_ref[...], mod_ref, 3).astype(BF16)
    _project(h, w_ref, wvt_ref, cos_ref, sin_ref, q_ref, k_ref, vt_ref, u_ref, ga_ref, gf_ref, offsets)


def _project(h, w_ref, wvt_ref, cos_ref, sin_ref, q_ref, k_ref, vt_ref, u_ref, ga_ref, gf_ref, offsets):
    cos = cos_ref[...]
    sin = sin_ref[...]
    lane = _lane_iota(h.shape[0])
    even_lane = (lane % 2) == 0
    low = lane < HEAD_DIM
    q_scale = HEAD_DIM ** -0.5
    rope = functools.partial(_rope, cos=cos, sin=sin, even_lane=even_lane)
    q_off, k_off, u_off, ga_off, gf_off = offsets

    for grp, p in _lane_groups(h, w_ref, q_off, q_ref.shape[1]):
        q_ref[:, grp * LANES:(grp + 1) * LANES] = (rope(p) * q_scale).astype(q_ref.dtype)
    _store_k(k_ref, h, w_ref, k_off, k_ref.shape[1] // 2, low, rope)
    _store_vt(vt_ref, h, wvt_ref)
    for grp, p in _lane_groups(h, w_ref, u_off, u_ref.shape[1]):
        u_ref[:, grp * LANES:(grp + 1) * LANES] = p.astype(u_ref.dtype)
    for ref, off in ((ga_ref, ga_off), (gf_ref, gf_off)):
        for grp, p in _lane_groups(h, w_ref, off, ref.shape[1]):
            ref[:, grp * LANES:(grp + 1) * LANES] = jax.nn.sigmoid(p).astype(ref.dtype)


def _in_projection(x, mod, g, w, w_vt, cos_t, sin_t, fourier_width, *, rows):
    b, n, d = x.shape
    q_w, kv_w = N_HEADS * HEAD_DIM, N_KV_HEADS * HEAD_DIM
    assert n % rows == 0 and rows % Q_BLOCK == 0 and kv_w % MXU_DIM == 0 and fourier_width % MXU_DIM == 0
    u_off = q_w + 2 * kv_w
    offsets = (0, q_w, u_off, u_off + fourier_width, u_off + fourier_width + d)
    kern = functools.partial(_inproj_kernel, offsets=offsets)
    tok = lambda width: pl.BlockSpec((None, rows, width), lambda si, bi: (bi, si, 0))
    key_blocks = rows // Q_BLOCK
    vt_rows = N_KV_HEADS * VT_ROWS
    out_widths = (q_w, 2 * kv_w, None, fourier_width, d, d)
    out_specs = [tok(width) if width else
                 pl.BlockSpec((None, key_blocks, vt_rows, Q_BLOCK), lambda si, bi: (bi, si, 0, 0))
                 for width in out_widths]
    out_shape = [jax.ShapeDtypeStruct((b, n, width), BF16) if width else
                 jax.ShapeDtypeStruct((b, n // Q_BLOCK, vt_rows, Q_BLOCK), BF16)
                 for width in out_widths]
    return pl.pallas_call(
        kern,
        grid=(n // rows, b),
        in_specs=[
            tok(d),
            pl.BlockSpec((None, N_MOD, d), lambda si, bi: (bi, 0, 0)),
            _resident((1, d)),
            _resident(w.shape),
            _resident(w_vt.shape),
            pl.BlockSpec((rows, LANES), lambda si, bi: (si, 0)),
            pl.BlockSpec((rows, LANES), lambda si, bi: (si, 0)),
        ],
        out_specs=out_specs,
        out_shape=out_shape,
        compiler_params=_params(2),
        name="in_projection",
    )(x, mod, g.reshape(1, d), w, w_vt, cos_t, sin_t)


def _ffn_in_projection(x, mod, g1, w1_in, w1_out, g, w, w_vt, cos_t, sin_t, fourier_width, *, rows):
    b, n, d = x.shape
    d_ff = w1_out.shape[0]
    q_w, kv_w = N_HEADS * HEAD_DIM, N_KV_HEADS * HEAD_DIM
    assert n % rows == 0 and rows % Q_BLOCK == 0 and d_ff % FF_CHUNK == 0
    u_off = q_w + 2 * kv_w
    offsets = (0, q_w, u_off, u_off + fourier_width, u_off + fourier_width + d)
    tok = lambda width: pl.BlockSpec((None, rows, width), lambda si, bi: (bi, si, 0))
    key_blocks = rows // Q_BLOCK
    vt_rows = N_KV_HEADS * VT_ROWS
    out_widths = (q_w, 2 * kv_w, None, fourier_width, d, d)
    out_specs = [tok(d)] + [tok(width) if width else
                            pl.BlockSpec((None, key_blocks, vt_rows, Q_BLOCK), lambda si, bi: (bi, si, 0, 0))
                            for width in out_widths]
    out_shape = [jax.ShapeDtypeStruct((b, n, d), F32)] + [
        jax.ShapeDtypeStruct((b, n, width), BF16) if width else
        jax.ShapeDtypeStruct((b, n // Q_BLOCK, vt_rows, Q_BLOCK), BF16) for width in out_widths]
    return pl.pallas_call(
        functools.partial(_ffn_inproj_kernel, offsets=offsets),
        grid=(n // rows, b),
        in_specs=[
            tok(d),
            pl.BlockSpec((None, N_MOD, d), lambda si, bi: (bi, 0, 0)),
            _resident((1, d)),
            _resident(w1_in.shape),
            _resident(w1_out.shape),
            _resident((1, d)),
            _resident(w.shape),
            _resident(w_vt.shape),
            pl.BlockSpec((rows, LANES), lambda si, bi: (si, 0)),
            pl.BlockSpec((rows, LANES), lambda si, bi: (si, 0)),
        ],
        out_specs=out_specs,
        out_shape=out_shape,
        compiler_params=_params(2),
        name="ffn_in_projection",
    )(x, mod, g1.reshape(1, d), w1_in, w1_out, g.reshape(1, d), w, w_vt, cos_t, sin_t)


def _transpose_kernel(w_ref, o_ref):
    o_ref[...] = w_ref[...].T.astype(o_ref.dtype)


def _transposed_columns(w, col, width):
    rows = w.shape[0]
    assert col % width == 0
    return pl.pallas_call(
        _transpose_kernel,
        grid=(1,),
        in_specs=[pl.BlockSpec((rows, width), lambda i: (0, col // width))],
        out_specs=pl.BlockSpec((width, rows), lambda i: (0, 0)),
        out_shape=jax.ShapeDtypeStruct((width, rows), BF16),
        compiler_params=_params(1),
        name="transpose_wv",
    )(w)


def _ctx_kv_kernel(x_ref, mod_ref, g_ref, w_ref, wvt_ref, k_ref, vt_ref, *, k_off):
    h = _adaln(x_ref[...], g_ref[...], mod_ref, 3).astype(BF16)
    low = _lane_iota(h.shape[0]) < HEAD_DIM
    _store_k(k_ref, h, w_ref, k_off, k_ref.shape[1] // 2, low, lambda p: p)
    _store_vt(vt_ref, h, wvt_ref)


def _ctx_kv(ctx, mod_ctx, g, w, w_vt):
    b, n, d = ctx.shape
    kv_w = N_KV_HEADS * HEAD_DIM
    vt_rows = N_KV_HEADS * VT_ROWS
    assert n % Q_BLOCK == 0
    kern = functools.partial(_ctx_kv_kernel, k_off=N_HEADS * HEAD_DIM)
    return pl.pallas_call(
        kern,
        grid=(b,),
        in_specs=[pl.BlockSpec((None, n, d), lambda bi: (bi, 0, 0)),
                  pl.BlockSpec((None, N_MOD, d), lambda bi: (0, 0, 0)),
                  _resident((1, d)), _resident(w.shape), _resident(w_vt.shape)],
        out_specs=[pl.BlockSpec((None, n, 2 * kv_w), lambda bi: (bi, 0, 0)),
                   pl.BlockSpec((None, n // Q_BLOCK, vt_rows, Q_BLOCK), lambda bi: (bi, 0, 0, 0))],
        out_shape=[jax.ShapeDtypeStruct((b, n, 2 * kv_w), BF16),
                   jax.ShapeDtypeStruct((b, n // Q_BLOCK, vt_rows, Q_BLOCK), BF16)],
        compiler_params=_params(1),
        name="ctx_kv",
    )(ctx, mod_ctx, g.reshape(1, d), w, w_vt)


def _attn_kernel(sink_ref, q_ref, k_ref, vt_ref, kc_ref, vct_ref, wa_ref, ga_ref, o_ref, *, seq):
    tile_start = pl.program_id(1) * ATTN_ROWS
    lane = _lane_iota(Q_BLOCK)
    keep_low = (lane < HEAD_DIM).astype(BF16)
    keep_high = 1.0 - keep_low
    key_row = lax.broadcasted_iota(jnp.int32, (KEY_SPAN, Q_BLOCK), 0)
    query_col = lax.broadcasted_iota(jnp.int32, (KEY_SPAN, Q_BLOCK), 1)
    head_of_lane = lax.broadcasted_iota(jnp.int32, (1, Q_GROUP * Q_BLOCK), 1) // Q_BLOCK
    n_ctx_blocks = vct_ref.shape[0]

    def window(sb):
        q_start = tile_start + sb * Q_BLOCK
        k_start = pl.multiple_of(jnp.clip(q_start - Q_BLOCK, 0, seq - KEY_SPAN), Q_BLOCK)
        dist = key_row - query_col + (k_start - q_start)
        bias = jnp.where(jnp.abs(dist) <= WINDOW, 0.0, NEG_INF).astype(F32)
        return pl.multiple_of(sb * Q_BLOCK, Q_BLOCK), k_start, bias

    def scores(win, j):
        r0, k_start, bias = win
        ks = slice(j * LANES, (j + 1) * LANES)
        qa = q_ref[pl.ds(r0, Q_BLOCK), 2 * j * LANES:(2 * j + 1) * LANES]
        qb = q_ref[pl.ds(r0, Q_BLOCK), (2 * j + 1) * LANES:(2 * j + 2) * LANES]
        q4 = jnp.concatenate([qa * keep_low, qa * keep_high, qb * keep_low, qb * keep_high], axis=0)
        keys = jnp.concatenate([k_ref[pl.ds(k_start, KEY_SPAN), ks], kc_ref[:, ks]], axis=0)
        s_all = _dot_nt(keys, q4)
        s_loc = jnp.concatenate([s_all[:KEY_SPAN, g * Q_BLOCK:(g + 1) * Q_BLOCK] + bias for g in range(Q_GROUP)],
                                axis=1)
        return s_loc.astype(BF16), s_all[KEY_SPAN:].astype(BF16)

    def block(step, carry):
        wins = [window(step * ATTN_UNROLL + i) for i in range(ATTN_UNROLL)]
        units = [(wi, j) for wi in range(ATTN_UNROLL) for j in range(N_KV_HEADS)]
        head_groups = {}
        piece = wa_ref.shape[1] // N_KV_HEADS

        def project(wi, c):
            attn_rows = jnp.concatenate(head_groups[wi], axis=1)
            rows, cols = pl.ds(wins[wi][0], Q_BLOCK), slice(c * piece, (c + 1) * piece)
            gated = ga_ref[rows, cols].astype(F32) * _dot(attn_rows, wa_ref[:, cols])
            o_ref[rows, cols] = gated.astype(o_ref.dtype)

        pending = [scores(wins[wi], j) for wi, j in units[:SCORE_LOOKAHEAD]]
        for idx, (wi, j) in enumerate(units):
            k_block = wins[wi][1] // Q_BLOCK
            s_loc, s_ctx = pending.pop(0)
            if idx + SCORE_LOOKAHEAD < len(units):
                nwi, nj = units[idx + SCORE_LOOKAHEAD]
                pending.append(scores(wins[nwi], nj))
            if wi > 0:
                project(wi - 1, j)
            vs = slice(j * VT_ROWS, (j + 1) * VT_ROWS)
            sink = jnp.zeros((1, Q_GROUP * Q_BLOCK), F32)
            for gi in range(Q_GROUP):
                sink = jnp.where(head_of_lane == gi, sink_ref[j * Q_GROUP + gi], sink)
            top = jnp.maximum(jnp.max(s_loc, axis=0, keepdims=True), jnp.max(s_ctx, axis=0, keepdims=True))
            m_b = jnp.maximum(top.astype(F32), sink).astype(BF16)
            m = m_b.astype(F32)
            p_all = jnp.concatenate([jnp.exp(s_loc - m_b), jnp.exp(s_ctx - m_b)], axis=0)
            vt_all = jnp.concatenate([vt_ref[k_block + t, vs, :] for t in range(KEY_BLOCKS)]
                                     + [vct_ref[t, vs, :] for t in range(n_ctx_blocks)], axis=1)
            out_t = _dot(vt_all, p_all)
            denom = out_t[HEAD_DIM:HEAD_DIM + 1, :] + jnp.exp(sink - m)
            vals = out_t[:HEAD_DIM, :] / denom
            for pair in range(Q_GROUP // 2):
                two_heads = jnp.concatenate([vals[:, (2 * pair) * Q_BLOCK:(2 * pair + 1) * Q_BLOCK],
                                             vals[:, (2 * pair + 1) * Q_BLOCK:(2 * pair + 2) * Q_BLOCK]], axis=0)
                head_groups.setdefault(wi, []).append(two_heads.T.astype(BF16))
        for c in range(N_KV_HEADS):
            project(ATTN_UNROLL - 1, c)
        return carry

    lax.fori_loop(0, ATTN_ROWS // (Q_BLOCK * ATTN_UNROLL), block, 0)


def _attention(sink, q, k2, vt, kc2, vct, w_a, ga):
    b, n, dq = q.shape
    n_ctx, d_out = kc2.shape[1], w_a.shape[1]
    assert n % ATTN_ROWS == 0 and n >= KEY_SPAN and n_ctx % Q_BLOCK == 0 and ATTN_ROWS % (Q_BLOCK * ATTN_UNROLL) == 0
    kern = functools.partial(_attn_kernel, seq=n)
    per_sample = lambda arr: pl.BlockSpec((None,) + arr.shape[1:], lambda bi, si: (bi,) + (0,) * (arr.ndim - 1))
    return pl.pallas_call(
        kern,
        grid=(b, n // ATTN_ROWS),
        in_specs=[
            pl.BlockSpec(memory_space=pltpu.SMEM),
            pl.BlockSpec((None, ATTN_ROWS, dq), lambda bi, si: (bi, si, 0)),
            per_sample(k2), per_sample(vt), per_sample(kc2), per_sample(vct),
            _resident(w_a.shape),
            pl.BlockSpec((None, ATTN_ROWS, d_out), lambda bi, si: (bi, si, 0)),
        ],
        out_specs=pl.BlockSpec((None, ATTN_ROWS, d_out), lambda bi, si: (bi, si, 0)),
        out_shape=jax.ShapeDtypeStruct((b, n, d_out), BF16),
        compiler_params=_params(2),
        name="attention",
    )(sink, q, k2, vt, kc2, vct, w_a, ga)


def _weighted_sum(terms, coefs, eps=1e-9):
    by_magnitude = {}
    for term, coef in zip(terms, coefs):
        if abs(coef) > eps:
            plus, minus = by_magnitude.setdefault(round(abs(coef), 9), ([], []))
            (plus if coef > 0 else minus).append(term)
    total = None
    for magnitude, (plus, minus) in by_magnitude.items():
        part = functools.reduce(lambda a, b: a + b, plus) if plus else None
        if minus:
            neg = functools.reduce(lambda a, b: a + b, minus)
            part = -neg if part is None else part - neg
        if abs(magnitude - 1.0) > eps:
            part = part * magnitude
        total = part if total is None else total + part
    return total


def _fourier_kernel(u_ref, tc_ref, ts_ref, wch_ref, wseq_ref, wf_ref, o_ref, v_scr, y_scr, *, n2):
    gd = FOURIER_GROUP_DIM
    d_out = wf_ref.shape[1]
    row_chunk = 256

    pairs = [(k1,) if (FFT_RADIX - k1) % FFT_RADIX == k1 else (k1, FFT_RADIX - k1)
             for k1 in range(FFT_RADIX // 2 + 1)]
    slot_of = {k1: (2 * idx + pos) % v_scr.shape[0] for idx, pair in enumerate(pairs) for pos, k1 in enumerate(pair)}

    def radix_stage(pair):
        k1 = pair[0]
        turns = [2.0 * np.pi * ((n1 * k1) % FFT_RADIX) / FFT_RADIX for n1 in range(FFT_RADIX)]
        re_coef = [float(np.cos(t)) for t in turns]
        im_coef = [-float(np.sin(t)) for t in turns]
        for g in range(N_FOURIER_GROUPS):
            for r0 in range(0, n2, row_chunk):
                us = [u_ref[n1 * n2 + r0:n1 * n2 + r0 + row_chunk, g * gd:(g + 1) * gd].astype(F32)
                      for n1 in range(FFT_RADIX)]
                er = _weighted_sum(us, re_coef)
                ei = _weighted_sum(us, im_coef)
                for pos, k in enumerate(pair):
                    ek = ei if pos == 0 or ei is None else -ei
                    if k == 0:
                        vr, vi = er, jnp.zeros_like(er)
                    else:
                        c = tc_ref[k - 1, r0:r0 + row_chunk, :]
                        s = ts_ref[k - 1, r0:r0 + row_chunk, :]
                        vr, vi = (er * c, -(er * s)) if ek is None else (er * c + ek * s, ek * c - er * s)
                    v_scr[slot_of[k], r0:r0 + row_chunk, 2 * g * gd:(2 * g + 1) * gd] = vr.astype(BF16)
                    v_scr[slot_of[k], r0:r0 + row_chunk, (2 * g + 1) * gd:(2 * g + 2) * gd] = vi.astype(BF16)

    def channel_stage(k1):
        slot = slot_of[k1]
        for g in range(N_FOURIER_GROUPS):
            y = _dot(v_scr[slot, :, 2 * g * gd:(2 * g + 2) * gd], wch_ref[...])
            y_scr[slot, 0:n2, g * gd:(g + 1) * gd] = y[:, :gd].astype(BF16)
            y_scr[slot, n2:2 * n2, g * gd:(g + 1) * gd] = y[:, gd:].astype(BF16)

    def sequence_stage(k1):
        f = _dot(wseq_ref[...], y_scr[slot_of[k1]]).astype(BF16)
        o_ref[:, k1 * d_out:(k1 + 1) * d_out] = _dot(f, wf_ref[...]).astype(o_ref.dtype)

    radix_stage(pairs[0])
    for idx, pair in enumerate(pairs):
        for pos, k1 in enumerate(pair):
            channel_stage(k1)
            if pos == 0 and idx + 1 < len(pairs):
                radix_stage(pairs[idx + 1])
            sequence_stage(k1)


def _fourier_tables(n):
    n2 = n // FFT_RADIX
    gd = FOURIER_GROUP_DIM
    pos = np.arange(n2, dtype=np.int64)
    k1 = np.arange(1, FFT_RADIX, dtype=np.int64)
    ang = 2.0 * np.pi * ((k1[:, None] * pos[None, :]) % n) / n
    tw_cos = np.broadcast_to(np.cos(ang)[:, :, None], (FFT_RADIX - 1, n2, LANES)).astype(np.float32)
    tw_sin = np.broadcast_to(np.sin(ang)[:, :, None], (FFT_RADIX - 1, n2, LANES)).astype(np.float32)
    ch = np.arange(gd, dtype=np.int64)
    ang_c = 2.0 * np.pi * ((ch[:, None] * ch[None, :]) % gd) / gd
    cc, sc = np.cos(ang_c), np.sin(ang_c)
    w_ch = np.block([[cc, -sc], [sc, cc]]).astype(np.float32)
    kk = np.arange(n2, dtype=np.int64)
    ang_s = 2.0 * np.pi * ((kk[:, None] * kk[None, :]) % n2) / n2
    scale = 1.0 / np.sqrt(float(n) * gd)
    w_seq = (np.concatenate([np.cos(ang_s), np.sin(ang_s)], axis=1) * scale).astype(np.float32)
    return jnp.asarray(tw_cos), jnp.asarray(tw_sin), jnp.asarray(w_ch), jnp.asarray(w_seq)


def _fourier(u, w_f):
    b, n, width = u.shape
    d_out = w_f.shape[1]
    assert n % (FFT_RADIX * 256) == 0 and width == N_FOURIER_GROUPS * FOURIER_GROUP_DIM
    n2 = n // FFT_RADIX
    tw_cos, tw_sin, w_ch, w_seq = _fourier_tables(n)
    kern = functools.partial(_fourier_kernel, n2=n2)
    return pl.pallas_call(
        kern,
        grid=(b,),
        in_specs=[
            pl.BlockSpec((None, n, width), lambda bi: (bi, 0, 0)),
            _resident(tw_cos.shape),
            _resident(tw_sin.shape),
            _resident(w_ch.shape),
            _resident(w_seq.shape),
            _resident(w_f.shape),
        ],
        out_specs=pl.BlockSpec((None, n2, FFT_RADIX * d_out), lambda bi: (bi, 0, 0)),
        out_shape=jax.ShapeDtypeStruct((b, n2, FFT_RADIX * d_out), BF16),
        scratch_shapes=[pltpu.VMEM((4, n2, 2 * width), BF16), pltpu.VMEM((4, 2 * n2, width), BF16)],
        compiler_params=_params(1),
        name="fourier",
    )(u, tw_cos, tw_sin, w_ch.astype(BF16), w_seq.astype(BF16), w_f)


def _rope_tables(n):
    rows = n // GRID_W
    half = HEAD_DIM // 2
    row = jnp.repeat(jnp.arange(rows, dtype=F32), GRID_W)
    col = jnp.tile(jnp.arange(GRID_W, dtype=F32), rows)
    inv_freq = ROPE_BASE ** (-jnp.arange(0, half, 2, dtype=F32) / half)
    ang = jnp.concatenate([row[:, None] * inv_freq, col[:, None] * inv_freq], axis=-1)
    cos, sin = jnp.cos(ang), jnp.sin(ang)
    reps = LANES // HEAD_DIM
    cos_t = jnp.tile(jnp.repeat(cos, 2, axis=-1), (1, reps))
    sin_t = jnp.tile(jnp.stack([-sin, sin], axis=-1).reshape(n, HEAD_DIM), (1, reps))
    return cos_t, sin_t


def kernel(x, c, ctx, c_ctx, w_ada, b_ada, g_ffn1, w_ffn1_in, w_ffn1_out, g_mix, w_in, attn_sink,
           w_attn_branch, w_fourier_branch, w_out, g_ffn2, w_ffn2_in, w_ffn2_out, g_final):
    b, n, d = x.shape
    depth = w_ada.shape[0]
    assert depth == 1, "single-layer problem: the last-layer path of the reference"
    layer = 0
    fourier_width = N_FOURIER_GROUPS * FOURIER_GROUP_DIM

    cond_rows = 8 * ((b + 1 + 7) // 8)
    cond_in = jnp.zeros((cond_rows, d), F32).at[:b].set(c).at[b].set(c_ctx)
    mod = _modulation(cond_in, w_ada[layer], b_ada[layer]).reshape(cond_rows, N_MOD, d)
    mod_lat, mod_ctx = mod[:b], mod[b:b + 1]

    bf = lambda w: w.astype(BF16)
    w_proj = bf(w_in[layer])
    v_off = (N_HEADS + N_KV_HEADS) * HEAD_DIM
    w_vt = _transposed_columns(w_in[layer], v_off, N_KV_HEADS * HEAD_DIM)
    cos_t, sin_t = _rope_tables(n)

    w1_in, w1_out = bf(w_ffn1_in[layer]), bf(w_ffn1_out[layer])
    x1, q, k2, vt, u, ga, gf = _ffn_in_projection(x, mod_lat, g_ffn1[layer], w1_in, w1_out, g_mix[layer], w_proj, w_vt,
                                                  cos_t, sin_t, fourier_width, rows=MIX_ROWS)
    ctx_flat = ctx.reshape(1, ctx.shape[0] * ctx.shape[1], d)
    ctx_rows = TOKEN_ROWS if ctx_flat.shape[1] % TOKEN_ROWS == 0 else ctx.shape[1]
    ctx1 = _ffn(ctx_flat, mod_ctx, g_ffn1[layer], w1_in, w1_out, base=0, rows=ctx_rows).reshape(ctx.shape)

    kc2, vct = _ctx_kv(ctx1, mod_ctx, g_mix[layer], w_proj, w_vt)

    ya = _attention(attn_sink[layer], q, k2, vt, kc2, vct, bf(w_attn_branch[layer]), ga)
    fw = _fourier(u, bf(w_fourier_branch[layer]))
    return _mix_ffn(x1, mod_lat, ya, fw, gf, bf(w_out[layer]), g_ffn2[layer], bf(w_ffn2_in[layer]),
                    bf(w_ffn2_out[layer]), g_final, base=6, rows=MIX_ROWS)
```

```python
import functools

import numpy as np
import jax
import jax.numpy as jnp
from jax import lax
from jax.experimental import pallas as pl
from jax.experimental.pallas import tpu as pltpu

N_HEADS = 16
N_KV_HEADS = 4
HEAD_DIM = 64
Q_GROUP = N_HEADS // N_KV_HEADS
WINDOW = 128
GRID_W = 64
N_FOURIER_GROUPS = 4
FOURIER_GROUP_DIM = 128
ROPE_BASE = 10000.0
N_MOD = 9
RMS_EPS = 1e-6
NEG_INF = -1e30

LANES = 128
MXU_DIM = 256
BF16_SUBLANES = 16
VMEM_LIMIT_BYTES = 56 * 1024 * 1024

TOKEN_ROWS = 1024
MIX_ROWS = 512
FF_CHUNK = MXU_DIM
ATTN_ROWS = 1024
Q_BLOCK = 128
ATTN_UNROLL = 8
SCORE_LOOKAHEAD = 2
KEY_BLOCKS = 3
KEY_SPAN = KEY_BLOCKS * Q_BLOCK
VT_ROWS = HEAD_DIM + BF16_SUBLANES
FFT_RADIX = 8

BF16 = jnp.bfloat16
F32 = jnp.float32


def _params(grid_rank):
    return pltpu.CompilerParams(dimension_semantics=("arbitrary",) * grid_rank, vmem_limit_bytes=VMEM_LIMIT_BYTES)


def _dot(a, b):
    return jnp.dot(a, b, preferred_element_type=F32)


def _dot_nt(a, b):
    return lax.dot_general(a, b, (((1,), (1,)), ((), ())), preferred_element_type=F32)


def _rms(x, g):
    return x * lax.rsqrt(jnp.mean(x * x, axis=-1, keepdims=True) + RMS_EPS) * g


def _adaln(x, g, mod_ref, base):
    shift = mod_ref[base:base + 1, :]
    scale = mod_ref[base + 1:base + 2, :]
    return _rms(x, g) * (1.0 + scale) + shift


def _resident(shape):
    zeros = (0,) * len(shape)
    return pl.BlockSpec(shape, lambda *_: zeros)


def _lane_iota(rows):
    return lax.broadcasted_iota(jnp.int32, (rows, LANES), 1)


def _mod_kernel(c_ref, w_ref, b_ref, o_ref):
    c = c_ref[...]
    cond = (c * jax.nn.sigmoid(c)).astype(BF16)
    o_ref[...] = _dot(cond, w_ref[...].astype(BF16)) + b_ref[...]


def _modulation(cond_in, w_ada, b_ada):
    rows, d = cond_in.shape
    n_out = w_ada.shape[1]
    return pl.pallas_call(
        _mod_kernel,
        grid=(n_out // d,),
        in_specs=[
            _resident((rows, d)),
            pl.BlockSpec((d, d), lambda j: (0, j)),
            pl.BlockSpec((1, d), lambda j: (0, j)),
        ],
        out_specs=pl.BlockSpec((rows, d), lambda j: (0, j)),
        out_shape=jax.ShapeDtypeStruct((rows, n_out), F32),
        compiler_params=_params(1),
        name="modulation",
    )(cond_in, w_ada, b_ada.reshape(1, n_out))


def _macaron_ffn(x, mod_ref, g_ref, win_ref, wout_ref, base):
    d_ff = wout_ref.shape[0]
    h = _adaln(x, g_ref[...], mod_ref, base).astype(BF16)
    acc = jnp.zeros(x.shape, F32)
    for c0 in range(0, d_ff, FF_CHUNK):
        a = _dot(h, win_ref[:, c0:c0 + FF_CHUNK])
        u = _dot(h, win_ref[:, d_ff + c0:d_ff + c0 + FF_CHUNK])
        act = (a * jax.nn.sigmoid(a) * u).astype(BF16)
        acc = acc + _dot(act, wout_ref[c0:c0 + FF_CHUNK, :])
    return x + 0.5 * mod_ref[base + 2:base + 3, :] * acc


def _ffn_kernel(x_ref, mod_ref, g_ref, win_ref, wout_ref, o_ref, *, base):
    o_ref[...] = _macaron_ffn(x_ref[...], mod_ref, g_ref, win_ref, wout_ref, base)


def _mix_ffn_kernel(x_ref, mod_ref, ya_ref, fw_ref, gf_ref, wo_ref, g_ref, win_ref, wout_ref, gfin_ref, o_ref,
                    fw_scr, *, base):
    quarter, d = fw_ref.shape[0], x_ref.shape[1]
    groups = fw_scr.shape[0]
    for k1 in range(FFT_RADIX):
        for grp in range(groups):
            cols = slice(k1 * d + grp * LANES, k1 * d + (grp + 1) * LANES)
            fw_scr[grp, pl.ds(k1, quarter, stride=FFT_RADIX), :] = fw_ref[:, cols].astype(F32)
    fw = jnp.concatenate([fw_scr[grp] for grp in range(groups)], axis=-1)
    y = ya_ref[...].astype(F32) + gf_ref[...].astype(F32) * fw
    x = x_ref[...] + mod_ref[5:6, :] * _dot(y.astype(BF16), wo_ref[...])
    out = _macaron_ffn(x, mod_ref, g_ref, win_ref, wout_ref, base)
    o_ref[...] = _rms(out, gfin_ref[...])


def _ffn(x, mod, g, w_in, w_out, *, base, rows):
    b, n, d = x.shape
    d_ff = w_out.shape[0]
    assert n % rows == 0 and d_ff % FF_CHUNK == 0
    per_batch_mod = mod.shape[0] == b
    mod_index = (lambda bi, si: (bi, 0, 0)) if per_batch_mod else (lambda bi, si: (0, 0, 0))
    return pl.pallas_call(
        functools.partial(_ffn_kernel, base=base),
        grid=(b, n // rows),
        in_specs=[
            pl.BlockSpec((None, rows, d), lambda bi, si: (bi, si, 0)),
            pl.BlockSpec((None, N_MOD, d), mod_index),
            _resident((1, d)),
            _resident((d, 2 * d_ff)),
            _resident((d_ff, d)),
        ],
        out_specs=pl.BlockSpec((None, rows, d), lambda bi, si: (bi, si, 0)),
        out_shape=jax.ShapeDtypeStruct((b, n, d), F32),
        compiler_params=_params(2),
        name="ffn",
    )(x, mod, g.reshape(1, d), w_in, w_out)


def _mix_ffn(x, mod, ya, fw, gf, w_o, g, w_in, w_out, g_final, *, base, rows):
    b, n, d = x.shape
    d_ff = w_out.shape[0]
    assert n % rows == 0 and rows % FFT_RADIX == 0 and d_ff % FF_CHUNK == 0 and fw.shape[2] == FFT_RADIX * d
    tok = lambda width: pl.BlockSpec((None, rows, width), lambda bi, si: (bi, si, 0))
    return pl.pallas_call(
        functools.partial(_mix_ffn_kernel, base=base),
        grid=(b, n // rows),
        in_specs=[
            tok(d),
            pl.BlockSpec((None, N_MOD, d), lambda bi, si: (bi, 0, 0)),
            tok(d),
            pl.BlockSpec((None, rows // FFT_RADIX, FFT_RADIX * d), lambda bi, si: (bi, si, 0)),
            tok(d),
            _resident(w_o.shape),
            _resident((1, d)),
            _resident((d, 2 * d_ff)),
            _resident((d_ff, d)),
            _resident((1, d)),
        ],
        out_specs=tok(d),
        out_shape=jax.ShapeDtypeStruct((b, n, d), F32),
        scratch_shapes=[pltpu.VMEM((d // LANES, rows, LANES), F32)],
        compiler_params=_params(2),
        name="mix_ffn_final",
    )(x, mod, ya, fw, gf, w_o, g.reshape(1, d), w_in, w_out, g_final.reshape(1, d))


def _rope(p, cos, sin, even_lane):
    partner = jnp.where(even_lane, pltpu.roll(p, LANES - 1, 1), pltpu.roll(p, 1, 1))
    return p * cos + partner * sin


def _duplicate_heads(pair, low):
    swapped = pltpu.roll(pair, HEAD_DIM, 1)
    return jnp.where(low, pair, swapped), jnp.where(low, swapped, pair)


def _lane_groups(h, w_ref, col, width):
    for c0 in range(0, width, MXU_DIM):
        p = _dot(h, w_ref[:, col + c0:col + c0 + MXU_DIM])
        for half in range(MXU_DIM // LANES):
            yield c0 // LANES + half, p[:, half * LANES:(half + 1) * LANES]


def _store_k(k_ref, h, w_ref, col, width, low, rope):
    for grp, p in _lane_groups(h, w_ref, col, width):
        first, second = _duplicate_heads(rope(p), low)
        k_ref[:, 2 * grp * LANES:(2 * grp + 1) * LANES] = first.astype(k_ref.dtype)
        k_ref[:, (2 * grp + 1) * LANES:(2 * grp + 2) * LANES] = second.astype(k_ref.dtype)


def _store_vt(vt_ref, h, wvt_ref):
    vt = _dot_nt(wvt_ref[...], h)
    ones = jnp.ones((VT_ROWS - HEAD_DIM, Q_BLOCK), vt_ref.dtype)
    for t in range(vt_ref.shape[0]):
        for j in range(N_KV_HEADS):
            vt_ref[t, j * VT_ROWS:j * VT_ROWS + HEAD_DIM, :] = (
                vt[j * HEAD_DIM:(j + 1) * HEAD_DIM, t * Q_BLOCK:(t + 1) * Q_BLOCK].astype(vt_ref.dtype))
            vt_ref[t, j * VT_ROWS + HEAD_DIM:(j + 1) * VT_ROWS, :] = ones


def _inproj_kernel(x_ref, mod_ref, g_ref, w_ref, wvt_ref, cos_ref, sin_ref,
                   q_ref, k_ref, vt_ref, u_ref, ga_ref, gf_ref, *, offsets):
    h = _adaln(x_ref[...], g_ref[...], mod_ref, 3).astype(BF16)
    cos = cos_ref[...]
    sin = sin_ref[...]
    lane = _lane_iota(h.shape[0])
    even_lane = (lane % 2) == 0
    low = lane < HEAD_DIM
    q_scale = HEAD_DIM ** -0.5
    rope = functools.partial(_rope, cos=cos, sin=sin, even_lane=even_lane)
    q_off, k_off, u_off, ga_off, gf_off = offsets

    for grp, p in _lane_groups(h, w_ref, q_off, q_ref.shape[1]):
        q_ref[:, grp * LANES:(grp + 1) * LANES] = (rope(p) * q_scale).astype(q_ref.dtype)
    _store_k(k_ref, h, w_ref, k_off, k_ref.shape[1] // 2, low, rope)
    _store_vt(vt_ref, h, wvt_ref)
    for grp, p in _lane_groups(h, w_ref, u_off, u_ref.shape[1]):
        u_ref[:, grp * LANES:(grp + 1) * LANES] = p.astype(u_ref.dtype)
    for ref, off in ((ga_ref, ga_off), (gf_ref, gf_off)):
        for grp, p in _lane_groups(h, w_ref, off, ref.shape[1]):
            ref[:, grp * LANES:(grp + 1) * LANES] = jax.nn.sigmoid(p).astype(ref.dtype)


def _in_projection(x, mod, g, w, w_vt, cos_t, sin_t, fourier_width, *, rows):
    b, n, d = x.shape
    q_w, kv_w = N_HEADS * HEAD_DIM, N_KV_HEADS * HEAD_DIM
    assert n % rows == 0 and rows % Q_BLOCK == 0 and kv_w % MXU_DIM == 0 and fourier_width % MXU_DIM == 0
    u_off = q_w + 2 * kv_w
    offsets = (0, q_w, u_off, u_off + fourier_width, u_off + fourier_width + d)
    kern = functools.partial(_inproj_kernel, offsets=offsets)
    tok = lambda width: pl.BlockSpec((None, rows, width), lambda si, bi: (bi, si, 0))
    key_blocks = rows // Q_BLOCK
    vt_rows = N_KV_HEADS * VT_ROWS
    out_widths = (q_w, 2 * kv_w, None, fourier_width, d, d)
    out_specs = [tok(width) if width else
                 pl.BlockSpec((None, key_blocks, vt_rows, Q_BLOCK), lambda si, bi: (bi, si, 0, 0))
                 for width in out_widths]
    out_shape = [jax.ShapeDtypeStruct((b, n, width), BF16) if width else
                 jax.ShapeDtypeStruct((b, n // Q_BLOCK, vt_rows, Q_BLOCK), BF16)
                 for width in out_widths]
    return pl.pallas_call(
        kern,
        grid=(n // rows, b),
        in_specs=[
            tok(d),
            pl.BlockSpec((None, N_MOD, d), lambda si, bi: (bi, 0, 0)),
            _resident((1, d)),
            _resident(w.shape),
            _resident(w_vt.shape),
            pl.BlockSpec((rows, LANES), lambda si, bi: (si, 0)),
            pl.BlockSpec((rows, LANES), lambda si, bi: (si, 0)),
        ],
        out_specs=out_specs,
        out_shape=out_shape,
        compiler_params=_params(2),
        name="in_projection",
    )(x, mod, g.reshape(1, d), w, w_vt, cos_t, sin_t)


def _transpose_kernel(w_ref, o_ref):
    o_ref[...] = w_ref[...].T.astype(o_ref.dtype)


def _transposed_columns(w, col, width):
    rows = w.shape[0]
    assert col % width == 0
    return pl.pallas_call(
        _transpose_kernel,
        grid=(1,),
        in_specs=[pl.BlockSpec((rows, width), lambda i: (0, col // width))],
        out_specs=pl.BlockSpec((width, rows), lambda i: (0, 0)),
        out_shape=jax.ShapeDtypeStruct((width, rows), BF16),
        compiler_params=_params(1),
        name="transpose_wv",
    )(w)


def _ctx_kv_kernel(x_ref, mod_ref, g_ref, w_ref, wvt_ref, k_ref, vt_ref, *, k_off):
    h = _adaln(x_ref[...], g_ref[...], mod_ref, 3).astype(BF16)
    low = _lane_iota(h.shape[0]) < HEAD_DIM
    _store_k(k_ref, h, w_ref, k_off, k_ref.shape[1] // 2, low, lambda p: p)
    _store_vt(vt_ref, h, wvt_ref)


def _ctx_kv(ctx, mod_ctx, g, w, w_vt):
    b, n, d = ctx.shape
    kv_w = N_KV_HEADS * HEAD_DIM
    vt_rows = N_KV_HEADS * VT_ROWS
    assert n % Q_BLOCK == 0
    kern = functools.partial(_ctx_kv_kernel, k_off=N_HEADS * HEAD_DIM)
    return pl.pallas_call(
        kern,
        grid=(b,),
        in_specs=[pl.BlockSpec((None, n, d), lambda bi: (bi, 0, 0)),
                  pl.BlockSpec((None, N_MOD, d), lambda bi: (0, 0, 0)),
                  _resident((1, d)), _resident(w.shape), _resident(w_vt.shape)],
        out_specs=[pl.BlockSpec((None, n, 2 * kv_w), lambda bi: (bi, 0, 0)),
                   pl.BlockSpec((None, n // Q_BLOCK, vt_rows, Q_BLOCK), lambda bi: (bi, 0, 0, 0))],
        out_shape=[jax.ShapeDtypeStruct((b, n, 2 * kv_w), BF16),
                   jax.ShapeDtypeStruct((b, n // Q_BLOCK, vt_rows, Q_BLOCK), BF16)],
        compiler_params=_params(1),
        name="ctx_kv",
    )(ctx, mod_ctx, g.reshape(1, d), w, w_vt)


def _attn_kernel(sink_ref, q_ref, k_ref, vt_ref, kc_ref, vct_ref, wa_ref, ga_ref, o_ref, *, seq):
    tile_start = pl.program_id(1) * ATTN_ROWS
    lane = _lane_iota(Q_BLOCK)
    keep_low = (lane < HEAD_DIM).astype(BF16)
    keep_high = 1.0 - keep_low
    key_row = lax.broadcasted_iota(jnp.int32, (KEY_SPAN, Q_BLOCK), 0)
    query_col = lax.broadcasted_iota(jnp.int32, (KEY_SPAN, Q_BLOCK), 1)
    head_of_lane = lax.broadcasted_iota(jnp.int32, (1, Q_GROUP * Q_BLOCK), 1) // Q_BLOCK
    n_ctx_blocks = vct_ref.shape[0]

    def window(sb):
        q_start = tile_start + sb * Q_BLOCK
        k_start = pl.multiple_of(jnp.clip(q_start - Q_BLOCK, 0, seq - KEY_SPAN), Q_BLOCK)
        dist = key_row - query_col + (k_start - q_start)
        bias = jnp.where(jnp.abs(dist) <= WINDOW, 0.0, NEG_INF).astype(F32)
        return pl.multiple_of(sb * Q_BLOCK, Q_BLOCK), k_start, bias

    def scores(win, j):
        r0, k_start, bias = win
        ks = slice(j * LANES, (j + 1) * LANES)
        qa = q_ref[pl.ds(r0, Q_BLOCK), 2 * j * LANES:(2 * j + 1) * LANES]
        qb = q_ref[pl.ds(r0, Q_BLOCK), (2 * j + 1) * LANES:(2 * j + 2) * LANES]
        q4 = jnp.concatenate([qa * keep_low, qa * keep_high, qb * keep_low, qb * keep_high], axis=0)
        keys = jnp.concatenate([k_ref[pl.ds(k_start, KEY_SPAN), ks], kc_ref[:, ks]], axis=0)
        s_all = _dot_nt(keys, q4)
        s_loc = jnp.concatenate([s_all[:KEY_SPAN, g * Q_BLOCK:(g + 1) * Q_BLOCK] + bias for g in range(Q_GROUP)],
                                axis=1)
        return s_loc.astype(BF16), s_all[KEY_SPAN:].astype(BF16)

    def block(step, carry):
        wins = [window(step * ATTN_UNROLL + i) for i in range(ATTN_UNROLL)]
        units = [(wi, j) for wi in range(ATTN_UNROLL) for j in range(N_KV_HEADS)]
        head_groups = {}
        piece = wa_ref.shape[1] // N_KV_HEADS

        def project(wi, c):
            attn_rows = jnp.concatenate(head_groups[wi], axis=1)
            rows, cols = pl.ds(wins[wi][0], Q_BLOCK), slice(c * piece, (c + 1) * piece)
            gated = ga_ref[rows, cols].astype(F32) * _dot(attn_rows, wa_ref[:, cols])
            o_ref[rows, cols] = gated.astype(o_ref.dtype)

        pending = [scores(wins[wi], j) for wi, j in units[:SCORE_LOOKAHEAD]]
        for idx, (wi, j) in enumerate(units):
            k_block = wins[wi][1] // Q_BLOCK
            s_loc, s_ctx = pending.pop(0)
            if idx + SCORE_LOOKAHEAD < len(units):
                nwi, nj = units[idx + SCORE_LOOKAHEAD]
                pending.append(scores(wins[nwi], nj))
            vs = slice(j * VT_ROWS, (j + 1) * VT_ROWS)
            sink = jnp.zeros((1, Q_GROUP * Q_BLOCK), F32)
            for gi in range(Q_GROUP):
                sink = jnp.where(head_of_lane == gi, sink_ref[j * Q_GROUP + gi], sink)
            top = jnp.maximum(jnp.max(s_loc, axis=0, keepdims=True), jnp.max(s_ctx, axis=0, keepdims=True))
            m_b = jnp.maximum(top.astype(F32), sink).astype(BF16)
            m = m_b.astype(F32)
            p_all = jnp.concatenate([jnp.exp(s_loc - m_b), jnp.exp(s_ctx - m_b)], axis=0)
            vt_all = jnp.concatenate([vt_ref[k_block + t, vs, :] for t in range(KEY_BLOCKS)]
                                     + [vct_ref[t, vs, :] for t in range(n_ctx_blocks)], axis=1)
            out_t = _dot(vt_all, p_all)
            if wi > 0:
                project(wi - 1, j)
            denom = out_t[HEAD_DIM:HEAD_DIM + 1, :] + jnp.exp(sink - m)
            vals = out_t[:HEAD_DIM, :] / denom
            for pair in range(Q_GROUP // 2):
                two_heads = jnp.concatenate([vals[:, (2 * pair) * Q_BLOCK:(2 * pair + 1) * Q_BLOCK],
                                             vals[:, (2 * pair + 1) * Q_BLOCK:(2 * pair + 2) * Q_BLOCK]], axis=0)
                head_groups.setdefault(wi, []).append(two_heads.T.astype(BF16))
        for c in range(N_KV_HEADS):
            project(ATTN_UNROLL - 1, c)
        return carry

    lax.fori_loop(0, ATTN_ROWS // (Q_BLOCK * ATTN_UNROLL), block, 0)


def _attention(sink, q, k2, vt, kc2, vct, w_a, ga):
    b, n, dq = q.shape
    n_ctx, d_out = kc2.shape[1], w_a.shape[1]
    assert n % ATTN_ROWS == 0 and n >= KEY_SPAN and n_ctx % Q_BLOCK == 0 and ATTN_ROWS % (Q_BLOCK * ATTN_UNROLL) == 0
    kern = functools.partial(_attn_kernel, seq=n)
    per_sample = lambda arr: pl.BlockSpec((None,) + arr.shape[1:], lambda bi, si: (bi,) + (0,) * (arr.ndim - 1))
    return pl.pallas_call(
        kern,
        grid=(b, n // ATTN_ROWS),
        in_specs=[
            pl.BlockSpec(memory_space=pltpu.SMEM),
            pl.BlockSpec((None, ATTN_ROWS, dq), lambda bi, si: (bi, si, 0)),
            per_sample(k2), per_sample(vt), per_sample(kc2), per_sample(vct),
            _resident(w_a.shape),
            pl.BlockSpec((None, ATTN_ROWS, d_out), lambda bi, si: (bi, si, 0)),
        ],
        out_specs=pl.BlockSpec((None, ATTN_ROWS, d_out), lambda bi, si: (bi, si, 0)),
        out_shape=jax.ShapeDtypeStruct((b, n, d_out), BF16),
        compiler_params=_params(2),
        name="attention",
    )(sink, q, k2, vt, kc2, vct, w_a, ga)


def _weighted_sum(terms, coefs, eps=1e-9):
    by_magnitude = {}
    for term, coef in zip(terms, coefs):
        if abs(coef) > eps:
            plus, minus = by_magnitude.setdefault(round(abs(coef), 9), ([], []))
            (plus if coef > 0 else minus).append(term)
    total = None
    for magnitude, (plus, minus) in by_magnitude.items():
        part = functools.reduce(lambda a, b: a + b, plus) if plus else None
        if minus:
            neg = functools.reduce(lambda a, b: a + b, minus)
            part = -neg if part is None else part - neg
        if abs(magnitude - 1.0) > eps:
            part = part * magnitude
        total = part if total is None else total + part
    return total


def _fourier_kernel(u_ref, tc_ref, ts_ref, wch_ref, wseq_ref, wf_ref, o_ref, v_scr, y_scr, *, n2):
    gd = FOURIER_GROUP_DIM
    d_out = wf_ref.shape[1]
    row_chunk = 256

    pairs = [(k1,) if (FFT_RADIX - k1) % FFT_RADIX == k1 else (k1, FFT_RADIX - k1)
             for k1 in range(FFT_RADIX // 2 + 1)]
    slot_of = {k1: (2 * idx + pos) % v_scr.shape[0] for idx, pair in enumerate(pairs) for pos, k1 in enumerate(pair)}

    def radix_stage(pair):
        k1 = pair[0]
        turns = [2.0 * np.pi * ((n1 * k1) % FFT_RADIX) / FFT_RADIX for n1 in range(FFT_RADIX)]
        re_coef = [float(np.cos(t)) for t in turns]
        im_coef = [-float(np.sin(t)) for t in turns]
        for g in range(N_FOURIER_GROUPS):
            for r0 in range(0, n2, row_chunk):
                us = [u_ref[n1 * n2 + r0:n1 * n2 + r0 + row_chunk, g * gd:(g + 1) * gd].astype(F32)
                      for n1 in range(FFT_RADIX)]
                er = _weighted_sum(us, re_coef)
                ei = _weighted_sum(us, im_coef)
                for pos, k in enumerate(pair):
                    ek = ei if pos == 0 or ei is None else -ei
                    if k == 0:
                        vr, vi = er, jnp.zeros_like(er)
                    else:
                        c = tc_ref[k - 1, r0:r0 + row_chunk, :]
                        s = ts_ref[k - 1, r0:r0 + row_chunk, :]
                        vr, vi = (er * c, -(er * s)) if ek is None else (er * c + ek * s, ek * c - er * s)
                    v_scr[slot_of[k], r0:r0 + row_chunk, 2 * g * gd:(2 * g + 1) * gd] = vr.astype(BF16)
                    v_scr[slot_of[k], r0:r0 + row_chunk, (2 * g + 1) * gd:(2 * g + 2) * gd] = vi.astype(BF16)

    def channel_stage(k1):
        slot = slot_of[k1]
        for g in range(N_FOURIER_GROUPS):
            y = _dot(v_scr[slot, :, 2 * g * gd:(2 * g + 2) * gd], wch_ref[...])
            y_scr[slot, 0:n2, g * gd:(g + 1) * gd] = y[:, :gd].astype(BF16)
            y_scr[slot, n2:2 * n2, g * gd:(g + 1) * gd] = y[:, gd:].astype(BF16)

    def sequence_stage(k1):
        f = _dot(wseq_ref[...], y_scr[slot_of[k1]]).astype(BF16)
        o_ref[:, k1 * d_out:(k1 + 1) * d_out] = _dot(f, wf_ref[...]).astype(o_ref.dtype)

    radix_stage(pairs[0])
    for idx, pair in enumerate(pairs):
        for pos, k1 in enumerate(pair):
            channel_stage(k1)
            if pos == 0 and idx + 1 < len(pairs):
                radix_stage(pairs[idx + 1])
            sequence_stage(k1)


def _fourier_tables(n):
    n2 = n // FFT_RADIX
    gd = FOURIER_GROUP_DIM
    pos = np.arange(n2, dtype=np.int64)
    k1 = np.arange(1, FFT_RADIX, dtype=np.int64)
    ang = 2.0 * np.pi * ((k1[:, None] * pos[None, :]) % n) / n
    tw_cos = np.broadcast_to(np.cos(ang)[:, :, None], (FFT_RADIX - 1, n2, LANES)).astype(np.float32)
    tw_sin = np.broadcast_to(np.sin(ang)[:, :, None], (FFT_RADIX - 1, n2, LANES)).astype(np.float32)
    ch = np.arange(gd, dtype=np.int64)
    ang_c = 2.0 * np.pi * ((ch[:, None] * ch[None, :]) % gd) / gd
    cc, sc = np.cos(ang_c), np.sin(ang_c)
    w_ch = np.block([[cc, -sc], [sc, cc]]).astype(np.float32)
    kk = np.arange(n2, dtype=np.int64)
    ang_s = 2.0 * np.pi * ((kk[:, None] * kk[None, :]) % n2) / n2
    scale = 1.0 / np.sqrt(float(n) * gd)
    w_seq = (np.concatenate([np.cos(ang_s), np.sin(ang_s)], axis=1) * scale).astype(np.float32)
    return jnp.asarray(tw_cos), jnp.asarray(tw_sin), jnp.asarray(w_ch), jnp.asarray(w_seq)


def _fourier(u, w_f):
    b, n, width = u.shape
    d_out = w_f.shape[1]
    assert n % (FFT_RADIX * 256) == 0 and width == N_FOURIER_GROUPS * FOURIER_GROUP_DIM
    n2 = n // FFT_RADIX
    tw_cos, tw_sin, w_ch, w_seq = _fourier_tables(n)
    kern = functools.partial(_fourier_kernel, n2=n2)
    return pl.pallas_call(
        kern,
        grid=(b,),
        in_specs=[
            pl.BlockSpec((None, n, width), lambda bi: (bi, 0, 0)),
            _resident(tw_cos.shape),
            _resident(tw_sin.shape),
            _resident(w_ch.shape),
            _resident(w_seq.shape),
            _resident(w_f.shape),
        ],
        out_specs=pl.BlockSpec((None, n2, FFT_RADIX * d_out), lambda bi: (bi, 0, 0)),
        out_shape=jax.ShapeDtypeStruct((b, n2, FFT_RADIX * d_out), BF16),
        scratch_shapes=[pltpu.VMEM((4, n2, 2 * width), BF16), pltpu.VMEM((4, 2 * n2, width), BF16)],
        compiler_params=_params(1),
        name="fourier",
    )(u, tw_cos, tw_sin, w_ch.astype(BF16), w_seq.astype(BF16), w_f)


def _rope_tables(n):
    rows = n // GRID_W
    half = HEAD_DIM // 2
    row = jnp.repeat(jnp.arange(rows, dtype=F32), GRID_W)
    col = jnp.tile(jnp.arange(GRID_W, dtype=F32), rows)
    inv_freq = ROPE_BASE ** (-jnp.arange(0, half, 2, dtype=F32) / half)
    ang = jnp.concatenate([row[:, None] * inv_freq, col[:, None] * inv_freq], axis=-1)
    cos, sin = jnp.cos(ang), jnp.sin(ang)
    reps = LANES // HEAD_DIM
    cos_t = jnp.tile(jnp.repeat(cos, 2, axis=-1), (1, reps))
    sin_t = jnp.tile(jnp.stack([-sin, sin], axis=-1).reshape(n, HEAD_DIM), (1, reps))
    return cos_t, sin_t


def kernel(x, c, ctx, c_ctx, w_ada, b_ada, g_ffn1, w_ffn1_in, w_ffn1_out, g_mix, w_in, attn_sink,
           w_attn_branch, w_fourier_branch, w_out, g_ffn2, w_ffn2_in, w_ffn2_out, g_final):
    b, n, d = x.shape
    depth = w_ada.shape[0]
    assert depth == 1, "single-layer problem: the last-layer path of the reference"
    layer = 0
    fourier_width = N_FOURIER_GROUPS * FOURIER_GROUP_DIM

    cond_rows = 8 * ((b + 1 + 7) // 8)
    cond_in = jnp.zeros((cond_rows, d), F32).at[:b].set(c).at[b].set(c_ctx)
    mod = _modulation(cond_in, w_ada[layer], b_ada[layer]).reshape(cond_rows, N_MOD, d)
    mod_lat, mod_ctx = mod[:b], mod[b:b + 1]

    bf = lambda w: w.astype(BF16)
    w_proj = bf(w_in[layer])
    v_off = (N_HEADS + N_KV_HEADS) * HEAD_DIM
    w_vt = _transposed_columns(w_in[layer], v_off, N_KV_HEADS * HEAD_DIM)
    cos_t, sin_t = _rope_tables(n)

    w1_in, w1_out = bf(w_ffn1_in[layer]), bf(w_ffn1_out[layer])
    x1 = _ffn(x, mod_lat, g_ffn1[layer], w1_in, w1_out, base=0, rows=TOKEN_ROWS)
    ctx_flat = ctx.reshape(1, ctx.shape[0] * ctx.shape[1], d)
    ctx_rows = TOKEN_ROWS if ctx_flat.shape[1] % TOKEN_ROWS == 0 else ctx.shape[1]
    ctx1 = _ffn(ctx_flat, mod_ctx, g_ffn1[layer], w1_in, w1_out, base=0, rows=ctx_rows).reshape(ctx.shape)

    q, k2, vt, u, ga, gf = _in_projection(x1, mod_lat, g_mix[layer], w_proj, w_vt, cos_t, sin_t, fourier_width,
                                          rows=TOKEN_ROWS)
    kc2, vct = _ctx_kv(ctx1, mod_ctx, g_mix[layer], w_proj, w_vt)

    ya = _attention(attn_sink[layer], q, k2, vt, kc2, vct, bf(w_attn_branch[layer]), ga)
    fw = _fourier(u, bf(w_fourier_branch[layer]))
    return _mix_ffn(x1, mod_lat, ya, fw, gf, bf(w_out[layer]), g_ffn2[layer], bf(w_ffn2_in[layer]),
                    bf(w_ffn2_out[layer]), g_final, base=6, rows=MIX_ROWS)
```
